```python
import math
import jax, jax.numpy as jnp
from jax import lax
import numpy as np

D_MODEL = 1024
BATCH = 32
SEQ = 2048
DEPTH = 1

CHUNK = 64
Q_BLOCK = 128

MIX_WIDTH = D_MODEL
HG_WIDTH = MIX_WIDTH // 2
HG_HEADS = 4
HG_DK = HG_WIDTH // HG_HEADS
FOX_WIDTH = MIX_WIDTH - HG_WIDTH
FOX_HEADS = 8
FOX_DH = FOX_WIDTH // FOX_HEADS
IN_COLS = 4 * HG_WIDTH + 3 * FOX_WIDTH + FOX_HEADS
N_EXPERT_GROUPS = 4
EXPERTS_PER_GROUP = 4
N_EXPERTS = N_EXPERT_GROUPS * EXPERTS_PER_GROUP
TOP_K = 2
EXPERT_HIDDEN = D_MODEL // 2
NORM_EPS = 1e-6
NEG_INF = -1e30

kernel_name = "hybrid_hgrn2_fox_hiermoe_block"


def rms_norm(x, gain):
    xf = x.astype(jnp.float32)
    y = xf * lax.rsqrt(jnp.mean(xf * xf, axis=-1, keepdims=True) + NORM_EPS)
    return (y * gain.astype(jnp.float32)).astype(x.dtype)


def hgrn2_mixer(q_raw, f_raw, i_raw, g_raw, lower_bound, norm_gain):
    B, S, _ = q_raw.shape
    nc = S // CHUNK
    f32 = jnp.float32
    q = jax.nn.silu(q_raw.astype(f32))
    lb = lower_bound.astype(f32)
    f = lb + (1.0 - lb) * jax.nn.sigmoid(f_raw.astype(f32))
    log_f = jnp.log(f)
    k = 1.0 - f
    v = i_raw.astype(f32)

    def to_chunks(t):
        return t.reshape(B, nc, CHUNK, HG_HEADS, HG_DK).transpose(1, 0, 3, 2, 4)

    causal = jnp.tril(jnp.ones((CHUNK, CHUNK), dtype=bool))

    def step(state, inp):
        qc, kc, vc, lfc = inp
        b = jnp.cumsum(lfc, axis=2)
        o_inter = jnp.einsum('bhtk,bhkv->bhtv', qc * jnp.exp(b), state)
        diff = b[:, :, :, None, :] - b[:, :, None, :, :]
        decay = jnp.where(causal[:, :, None], jnp.exp(jnp.minimum(diff, 0.0)), 0.0)
        scores = jnp.einsum('bhtk,bhtsk,bhsk->bhts', qc, decay, kc)
        o_intra = jnp.einsum('bhts,bhsv->bhtv', scores, vc)
        b_last = b[:, :, -1:, :]
        new_state = (jnp.exp(b_last[:, :, 0, :])[..., None] * state
                     + jnp.einsum('bhsk,bhsv->bhkv', kc * jnp.exp(b_last - b), vc))
        return new_state, o_inter + o_intra

    s0 = jnp.zeros((B, HG_HEADS, HG_DK, HG_DK), f32)
    _, o = lax.scan(step, s0, (to_chunks(q), to_chunks(k), to_chunks(v), to_chunks(log_f)))
    o = o.transpose(1, 0, 3, 2, 4).reshape(B, S, HG_HEADS, HG_DK)
    o = rms_norm(o, norm_gain)
    o = o.reshape(B, S, HG_WIDTH) * jax.nn.silu(g_raw.astype(f32))
    return o.astype(q_raw.dtype)


def fox_mixer(q_raw, k_raw, v_raw, f_raw, f_bias, norm_gain):
    B, S, _ = q_raw.shape
    f32 = jnp.float32

    def heads(t):
        return t.astype(f32).reshape(B, S, FOX_HEADS, FOX_DH).transpose(0, 2, 1, 3)

    q = heads(q_raw) * (FOX_DH ** -0.5)
    k = heads(k_raw)
    v = heads(v_raw)
    log_f = jax.nn.log_sigmoid(f_raw.astype(f32) + f_bias.astype(f32))
    cum = jnp.cumsum(log_f, axis=1).transpose(0, 2, 1)
    key_pos = jnp.arange(S)

    def block(i):
        start = i * Q_BLOCK
        qb = lax.dynamic_slice_in_dim(q, start, Q_BLOCK, axis=2)
        cb = lax.dynamic_slice_in_dim(cum, start, Q_BLOCK, axis=2)
        logits = jnp.einsum('bhtd,bhsd->bhts', qb, k) + (cb[..., :, None] - cum[..., None, :])
        mask = (start + jnp.arange(Q_BLOCK))[:, None] >= key_pos[None, :]
        logits = jnp.where(mask, logits, NEG_INF)
        p = jax.nn.softmax(logits, axis=-1)
        return jnp.einsum('bhts,bhsd->bhtd', p, v)

    o = lax.map(block, jnp.arange(S // Q_BLOCK))
    o = o.transpose(1, 0, 3, 2, 4).reshape(B, S, FOX_HEADS, FOX_DH)
    o = rms_norm(o, norm_gain)
    return o.reshape(B, S, FOX_WIDTH).astype(q_raw.dtype)


def hier_moe(x, w_group, b_group, w_expert, b_expert, w_gate, w_up, w_down):
    B, S, D = x.shape
    t = x.reshape(-1, D)
    pg = jax.nn.softmax((t @ w_group + b_group).astype(jnp.float32), axis=-1)
    gsel = jnp.argmax(pg, axis=-1)
    pg_sel = jnp.max(pg, axis=-1)
    e_logits = (t @ w_expert + b_expert).astype(jnp.float32).reshape(-1, N_EXPERT_GROUPS, EXPERTS_PER_GROUP)
    e_logits = jnp.take_along_axis(e_logits, gsel[:, None, None], axis=1)[:, 0]
    pe = jax.nn.softmax(e_logits, axis=-1)
    top_p, top_i = lax.top_k(pe, TOP_K)
    top_p = top_p / jnp.sum(top_p, axis=-1, keepdims=True)
    expert_id = gsel[:, None] * EXPERTS_PER_GROUP + top_i
    weights = pg_sel[:, None] * top_p
    gates = jnp.sum(jax.nn.one_hot(expert_id, N_EXPERTS, dtype=jnp.float32) * weights[..., None], axis=1)
    gates = gates.astype(t.dtype)
    y = jnp.zeros_like(t)
    for e in range(N_EXPERTS):
        h = jax.nn.silu(t @ w_gate[e]) * (t @ w_up[e])
        y = y + gates[:, e:e + 1] * (h @ w_down[e])
    return y.reshape(B, S, D)


def setup_inputs(seed: int = 0) -> dict:
    key = jax.random.key(seed)
    ks = jax.random.split(key, 20)
    f32 = jnp.float32
    L, D = DEPTH, D_MODEL
    nrm = lambda k, shape, scale: jax.random.normal(k, shape, f32) * scale
    return {
        "x": jax.random.normal(ks[0], (BATCH, SEQ, D), f32),
        "attn_norm": 1.0 + nrm(ks[1], (L, D), 0.02),
        "w_in": nrm(ks[2], (L, D, IN_COLS), D ** -0.5),
        "hg_lb_logits": nrm(ks[3], (L + 1, HG_WIDTH), 0.5),
        "hg_norm": 1.0 + nrm(ks[4], (L, HG_DK), 0.02),
        "fox_f_bias": nrm(ks[5], (L, FOX_HEADS), 0.1),
        "fox_norm": 1.0 + nrm(ks[6], (L, FOX_DH), 0.02),
        "w_out": nrm(ks[7], (L, MIX_WIDTH, D), MIX_WIDTH ** -0.5),
        "ffn_norm": 1.0 + nrm(ks[8], (L, D), 0.02),
        "w_group": nrm(ks[9], (L, D, N_EXPERT_GROUPS), D ** -0.5),
        "b_group": nrm(ks[10], (L, N_EXPERT_GROUPS), 0.01),
        "w_expert": nrm(ks[11], (L, D, N_EXPERTS), D ** -0.5),
        "b_expert": nrm(ks[12], (L, N_EXPERTS), 0.01),
        "w_gate": nrm(ks[13], (L, N_EXPERTS, D, EXPERT_HIDDEN), D ** -0.5),
        "w_up": nrm(ks[14], (L, N_EXPERTS, D, EXPERT_HIDDEN), D ** -0.5),
        "w_down": nrm(ks[15], (L, N_EXPERTS, EXPERT_HIDDEN, D), EXPERT_HIDDEN ** -0.5),
        "final_norm": 1.0 + nrm(ks[16], (D,), 0.02),
    }


def reference(x, attn_norm, w_in, hg_lb_logits, hg_norm, fox_f_bias, fox_norm, w_out, ffn_norm,
              w_group, b_group, w_expert, b_expert, w_gate, w_up, w_down, final_norm):
    lb_all = jnp.cumsum(jax.nn.softmax(hg_lb_logits.astype(jnp.float32), axis=0), axis=0)
    o0 = 0
    o1 = o0 + HG_WIDTH
    o2 = o1 + HG_WIDTH
    o3 = o2 + HG_WIDTH
    o4 = o3 + HG_WIDTH
    o5 = o4 + FOX_WIDTH
    o6 = o5 + FOX_WIDTH
    o7 = o6 + FOX_WIDTH
    h = x
    for l in range(DEPTH):
        xn = rms_norm(h, attn_norm[l])
        u = xn @ w_in[l]
        o_a = hgrn2_mixer(u[..., o0:o1], u[..., o1:o2], u[..., o2:o3], u[..., o3:o4],
                          lb_all[l], hg_norm[l])
        o_b = fox_mixer(u[..., o4:o5], u[..., o5:o6], u[..., o6:o7], u[..., o7:IN_COLS],
                        fox_f_bias[l], fox_norm[l])
        h = h + jnp.concatenate([o_a, o_b], axis=-1) @ w_out[l]
        h = h + hier_moe(rms_norm(h, ffn_norm[l]), w_group[l], b_group[l], w_expert[l], b_expert[l],
                         w_gate[l], w_up[l], w_down[l])
    return rms_norm(h, final_norm)
```

```python
import functools

import jax
import jax.numpy as jnp
from jax import lax
from jax.experimental import pallas as pl
from jax.experimental.pallas import tpu as pltpu

F32 = jnp.float32
BF16 = jnp.bfloat16

NORM_EPS = 1e-6
NEG_INF = -1e30

LANES = 128
HG_HEADS = 4
HG_DK = 128
FOX_HEADS = 8
FOX_DH = 64
N_GROUPS = 4
PER_GROUP = 4
N_EXPERTS = 16

SUB = 16
SUPER = 128
VMEM_LIMIT = 56 * 1024 * 1024


def _cparams(sem):
    return pltpu.CompilerParams(dimension_semantics=sem, vmem_limit_bytes=VMEM_LIMIT)


def _split3(x):
    hi = x.astype(BF16)
    r1 = x - hi.astype(F32)
    mid = r1.astype(BF16)
    lo = (r1 - mid.astype(F32)).astype(BF16)
    return hi, mid, lo


def _dot(a, b):
    return jnp.dot(a, b, preferred_element_type=F32)


def _dot_nt(a, b):
    return lax.dot_general(a, b, (((1,), (1,)), ((), ())), preferred_element_type=F32)


def _dot3(m_bf16, x_f32):
    hi, mid, lo = _split3(x_f32)
    return _dot(m_bf16, hi) + _dot(m_bf16, mid) + _dot(m_bf16, lo)


def _inproj_kernel(x_ref, g_ref, w_ref, u_ref, hgf_ref, foxf_ref, xn_ref, *, n_main, col_chunk):
    x = x_ref[...]
    ms = jnp.mean(x * x, axis=-1, keepdims=True)
    xn_ref[...] = ((x * lax.rsqrt(ms + NORM_EPS)) * g_ref[...]).astype(BF16)
    for j in range(n_main // col_chunk):
        r = _dot(xn_ref[...], w_ref[:, j * col_chunk:(j + 1) * col_chunk])
        u_ref[:, j * col_chunk:(j + 1) * col_chunk] = r.astype(BF16)
    r = _dot(xn_ref[...], w_ref[:, n_main:n_main + 4 * LANES])
    hgf_ref[...] = r
    r = _dot(xn_ref[...], w_ref[:, n_main + 4 * LANES:n_main + 5 * LANES])
    foxf_ref[0] = r.T[0:FOX_HEADS, :]


def _inproj(x2, attn_norm, w_all, B, S, tm):
    T, D = x2.shape
    n_main = 6 * 512
    per_seq = S // tm
    return pl.pallas_call(
        functools.partial(_inproj_kernel, n_main=n_main, col_chunk=512),
        grid=(T // tm,),
        in_specs=[
            pl.BlockSpec((tm, D), lambda i: (i, 0)),
            pl.BlockSpec((1, D), lambda i: (0, 0)),
            pl.BlockSpec(w_all.shape, lambda i: (0, 0)),
        ],
        out_specs=[
            pl.BlockSpec((tm, n_main), lambda i: (i, 0)),
            pl.BlockSpec((tm, 4 * LANES), lambda i: (i, 0)),
            pl.BlockSpec((1, FOX_HEADS, tm), lambda i: (i // per_seq, 0, i % per_seq)),
        ],
        out_shape=[
            jax.ShapeDtypeStruct((T, n_main), BF16),
            jax.ShapeDtypeStruct((T, 4 * LANES), F32),
            jax.ShapeDtypeStruct((B, FOX_HEADS, S), F32),
        ],
        scratch_shapes=[pltpu.VMEM((tm, D), BF16)],
        compiler_params=_cparams(("arbitrary",)),
        name="inproj",
    )(x2, attn_norm, w_all)


def _foxcum_kernel(f_ref, bias_ref, cum_ref, off_ref):
    blk = f_ref.shape[1]

    @pl.when(pl.program_id(0) == 0)
    def _():
        off_ref[...] = jnp.zeros_like(off_ref)

    r_i = lax.broadcasted_iota(jnp.int32, (blk, blk), 0)
    c_i = lax.broadcasted_iota(jnp.int32, (blk, blk), 1)
    upper = (r_i <= c_i).astype(BF16)
    z = f_ref[...] + bias_ref[...]
    logf = jnp.minimum(z, 0.0) - jnp.log(1.0 + jnp.exp(-jnp.abs(z)))
    hi, mid, lo = _split3(logf)
    cum = _dot(hi, upper) + _dot(mid, upper) + _dot(lo, upper) + off_ref[...]
    cum_ref[...] = cum
    off_ref[...] = cum[:, blk - 1:blk]


def _foxcum(foxf2, bias_col, blk):
    R, S = foxf2.shape
    return pl.pallas_call(
        _foxcum_kernel,
        grid=(S // blk,),
        in_specs=[pl.BlockSpec((R, blk), lambda j: (0, j)), pl.BlockSpec((R, 1), lambda j: (0, 0))],
        out_specs=pl.BlockSpec((R, blk), lambda j: (0, j)),
        out_shape=jax.ShapeDtypeStruct((R, S), F32),
        scratch_shapes=[pltpu.VMEM((R, 1), F32)],
        compiler_params=_cparams(("arbitrary",)),
        name="foxcum",
    )(foxf2, bias_col)


def _fox_kernel(q_ref, k_ref, v_ref, cum_ref, gain_ref, o_ref, *, tq):
    qi = pl.program_id(2)
    lane = lax.broadcasted_iota(jnp.int32, (1, LANES), 1)
    in_h = [lane < FOX_DH, lane >= FOX_DH]
    q = q_ref[0] * jnp.asarray(FOX_DH ** -0.5, BF16)
    qh = [jnp.where(in_h[h], q, jnp.zeros_like(q)) for h in range(2)]
    row = lax.broadcasted_iota(jnp.int32, (tq, tq), 0)
    col = lax.broadcasted_iota(jnp.int32, (tq, tq), 1)
    causal = row >= col

    def step(kb, carry, masked):
        k_blk = k_ref[0, pl.ds(pl.multiple_of(kb * tq, tq), tq), :]
        v_blk = v_ref[0, pl.ds(pl.multiple_of(kb * tq, tq), tq), :]
        out = []
        for h in range(2):
            m, acc = carry[2 * h], carry[2 * h + 1]
            bias = -cum_ref[0, 0, h, pl.ds(kb, 1), :]
            s = _dot_nt(qh[h], k_blk) + bias
            if masked:
                s = jnp.where(causal, s, NEG_INF)
            m_new = jnp.maximum(m, jnp.max(s, axis=-1, keepdims=True))
            p = jnp.exp(s - m_new).astype(BF16)
            vh = jnp.where(in_h[h], v_blk, jnp.ones_like(v_blk))
            acc = acc * jnp.exp(m - m_new) + _dot(p, vh)
            out += [m_new, acc]
        return tuple(out)

    init = (jnp.full((tq, 1), NEG_INF, F32), jnp.zeros((tq, LANES), F32)) * 2
    carry = lax.fori_loop(0, qi, lambda kb, c: step(kb, c, False), init)
    carry = step(qi, carry, True)
    o0 = carry[1] / pltpu.roll(carry[1], FOX_DH, 1)
    o1 = carry[3] / pltpu.roll(carry[3], FOX_DH, 1)
    o = jnp.where(in_h[0], o0, o1)
    o2 = o * o
    ms0 = jnp.sum(jnp.where(in_h[0], o2, 0.0), axis=-1, keepdims=True) * (1.0 / FOX_DH)
    ms1 = jnp.sum(jnp.where(in_h[1], o2, 0.0), axis=-1, keepdims=True) * (1.0 / FOX_DH)
    ms = jnp.where(in_h[0], ms0, ms1)
    o_ref[0] = (o * lax.rsqrt(ms + NORM_EPS) * gain_ref[...]).astype(BF16)


def _fox(u3, cum5, gain2, tq):
    B, S, _ = u3.shape
    nq = S // tq
    qc, kc, vc = 12, 16, 20
    return pl.pallas_call(
        functools.partial(_fox_kernel, tq=tq),
        grid=(B, FOX_HEADS // 2, nq),
        in_specs=[
            pl.BlockSpec((1, tq, LANES), lambda b, p, i: (b, i, qc + p)),
            pl.BlockSpec((1, S, LANES), lambda b, p, i: (b, 0, kc + p)),
            pl.BlockSpec((1, S, LANES), lambda b, p, i: (b, 0, vc + p)),
            pl.BlockSpec((1, 1, 2, nq, tq), lambda b, p, i: (b, p, 0, 0, 0)),
            pl.BlockSpec((1, LANES), lambda b, p, i: (0, 0)),
        ],
        out_specs=pl.BlockSpec((1, tq, LANES), lambda b, p, i: (b, i, p)),
        out_shape=jax.ShapeDtypeStruct((B, S, FOX_HEADS * FOX_DH), BF16),
        compiler_params=_cparams(("arbitrary", "arbitrary", "arbitrary")),
        name="fox",
    )(u3, u3, u3, cum5, gain2)


def _hgrn2_kernel(q_ref, i_ref, g_ref, f_ref, lbl_ref, gain_ref, o_ref, b_scr, q_scr, k_scr, st_scr):
    S = q_ref.shape[1]
    n_sub = SUPER // SUB
    a = lbl_ref[...]
    am = jnp.max(a, axis=0, keepdims=True)
    ea = jnp.exp(a - am)
    lb = ea[0:1, :] / (ea[0:1, :] + ea[1:2, :])

    r_i = lax.broadcasted_iota(jnp.int32, (SUPER, SUPER), 0)
    c_i = lax.broadcasted_iota(jnp.int32, (SUPER, SUPER), 1)
    same = (r_i // SUB) == (c_i // SUB)
    tri = (same & (c_i <= r_i)).astype(BF16)
    blk1 = same.astype(BF16)
    ones = jnp.ones((LANES, LANES), BF16)
    lane = lax.broadcasted_iota(jnp.int32, (1, LANES), 1)
    trow = lax.broadcasted_iota(jnp.int32, (8, LANES), 0)

    st_scr[...] = jnp.zeros_like(st_scr)

    def superblock(sb, _):
        r0 = pl.multiple_of(sb * SUPER, SUPER)
        qr = q_ref[0, pl.ds(r0, SUPER), :].astype(F32)
        q = qr * jax.nn.sigmoid(qr)
        f = lb + (1.0 - lb) * jax.nn.sigmoid(f_ref[0, pl.ds(r0, SUPER), :])
        logf = jnp.log(f)
        k = 1.0 - f
        v = i_ref[0, pl.ds(r0, SUPER), :]
        b = _dot3(tri, logf)
        btot = _dot3(blk1, logf)
        qt = (q * jnp.exp(b)).astype(BF16)
        kh = (k * jnp.exp(btot - b)).astype(BF16)
        gam = jnp.exp(btot)
        b_scr[...] = b
        q_scr[...] = q
        k_scr[...] = k
        vt = v.astype(F32).T

        st = st_scr[...]
        o_inter = []
        for c in range(n_sub):
            o_inter.append(_dot_nt(qt[c * SUB:(c + 1) * SUB, :], st.astype(BF16)))
            vtm = jnp.where((lane // SUB) == c, vt, 0.0).astype(BF16)
            st = gam[c * SUB:c * SUB + 1, :] * st + _dot(vtm, kh)
        st_scr[...] = st

        a_rows = []
        for c in range(n_sub):
            base = c * SUB
            tiles = []
            lo_half = []
            for j in range(SUB):
                bs = b_scr[base + j:base + j + 1, :]
                ks = k_scr[base + j:base + j + 1, :]
                halves = []
                for hf in range(2):
                    if hf == 0 and j >= 8:
                        continue
                    t0 = base + 8 * hf
                    p = (q_scr[t0:t0 + 8, :] * jnp.exp(b_scr[t0:t0 + 8, :] - bs)) * ks
                    if j // 8 == hf:
                        p = jnp.where(trow >= (j % 8), p, 0.0)
                    halves.append(p)
                if j < 8:
                    tiles.append(jnp.concatenate(halves, axis=0))
                else:
                    lo_half.append(halves[0])
            for m in range(4):
                tiles.append(jnp.concatenate([lo_half[2 * m], lo_half[2 * m + 1]], axis=0))
            pc = jnp.concatenate(tiles, axis=0).astype(BF16)
            rc = _dot(pc, ones)
            a_top = jnp.zeros((8, LANES), F32)
            a_bot = jnp.zeros((8, LANES), F32)
            for j in range(8):
                sel = lane == (base + j)
                a_top = jnp.where(sel, rc[16 * j:16 * j + 8, :], a_top)
                a_bot = jnp.where(sel, rc[16 * j + 8:16 * j + 16, :], a_bot)
            for j in range(8, SUB):
                sel = lane == (base + j)
                a_bot = jnp.where(sel, rc[128 + 8 * (j - 8):128 + 8 * (j - 8) + 8, :], a_bot)
            a_rows += [a_top, a_bot]
        a_blk = jnp.concatenate(a_rows, axis=0).astype(BF16)
        o = jnp.concatenate(o_inter, axis=0) + _dot(a_blk, v)
        ms = jnp.mean(o * o, axis=-1, keepdims=True)
        o = o * lax.rsqrt(ms + NORM_EPS) * gain_ref[...]
        gr = g_ref[0, pl.ds(r0, SUPER), :].astype(F32)
        o_ref[0, pl.ds(r0, SUPER), :] = (o * (gr * jax.nn.sigmoid(gr))).astype(BF16)
        return 0

    lax.fori_loop(0, S // SUPER, superblock, 0)


def _hgrn2(u3, hgf3, lb_logits, gain):
    B, S, _ = u3.shape
    return pl.pallas_call(
        _hgrn2_kernel,
        grid=(B, HG_HEADS),
        in_specs=[
            pl.BlockSpec((1, S, LANES), lambda b, h: (b, 0, h)),
            pl.BlockSpec((1, S, LANES), lambda b, h: (b, 0, 4 + h)),
            pl.BlockSpec((1, S, LANES), lambda b, h: (b, 0, 8 + h)),
            pl.BlockSpec((1, S, LANES), lambda b, h: (b, 0, h)),
            pl.BlockSpec((2, LANES), lambda b, h: (0, h)),
            pl.BlockSpec((1, LANES), lambda b, h: (0, 0)),
        ],
        out_specs=pl.BlockSpec((1, S, LANES), lambda b, h: (b, 0, h)),
        out_shape=jax.ShapeDtypeStruct((B, S, HG_HEADS * HG_DK), BF16),
        scratch_shapes=[pltpu.VMEM((SUPER, LANES), F32)] * 4,
        compiler_params=_cparams(("arbitrary", "arbitrary")),
        name="hgrn2",
    )(u3, u3, u3, hgf3, lb_logits, gain)


def _outproj_kernel(oa_ref, ob_ref, x_ref, wo_ref, g_ref, wr_ref, br_ref, h_ref, hn_ref, gates_ref):
    half = oa_ref.shape[1]
    h = x_ref[...] + _dot(oa_ref[...], wo_ref[0:half, :]) + _dot(ob_ref[...], wo_ref[half:2 * half, :])
    h_ref[...] = h
    ms = jnp.mean(h * h, axis=-1, keepdims=True)
    hn = (h * lax.rsqrt(ms + NORM_EPS)) * g_ref[...]
    hn_ref[...] = hn.astype(BF16)
    hn_hi = hn.astype(BF16)
    hn_lo = (hn - hn_hi.astype(F32)).astype(BF16)
    wr = wr_ref[...]
    wr_hi = wr.astype(BF16)
    wr_lo = (wr - wr_hi.astype(F32)).astype(BF16)
    logits = _dot(hn_hi, wr_hi) + _dot(hn_lo, wr_hi) + _dot(hn_hi, wr_lo) + br_ref[...]
    lt = logits.T
    g = [lt[i:i + 1, :] for i in range(N_GROUPS)]
    gm = jnp.maximum(jnp.maximum(g[0], g[1]), jnp.maximum(g[2], g[3]))
    gsel = jnp.where(g[0] == gm, 0, jnp.where(g[1] == gm, 1, jnp.where(g[2] == gm, 2, 3)))
    pg = 1.0 / (jnp.exp(g[0] - gm) + jnp.exp(g[1] - gm) + jnp.exp(g[2] - gm) + jnp.exp(g[3] - gm))
    e = []
    for i in range(PER_GROUP):
        rows = [lt[N_GROUPS + PER_GROUP * gg + i:N_GROUPS + PER_GROUP * gg + i + 1, :] for gg in range(N_GROUPS)]
        e.append(jnp.where(gsel == 0, rows[0], jnp.where(gsel == 1, rows[1], jnp.where(gsel == 2, rows[2], rows[3]))))
    e1 = jnp.maximum(jnp.maximum(e[0], e[1]), jnp.maximum(e[2], e[3]))
    i1 = jnp.where(e[0] == e1, 0, jnp.where(e[1] == e1, 1, jnp.where(e[2] == e1, 2, 3)))
    ex = [jnp.where(i1 == i, -jnp.inf, e[i]) for i in range(PER_GROUP)]
    e2 = jnp.maximum(jnp.maximum(ex[0], ex[1]), jnp.maximum(ex[2], ex[3]))
    i2 = jnp.where(ex[0] == e2, 0, jnp.where(ex[1] == e2, 1, jnp.where(ex[2] == e2, 2, 3)))
    r = jnp.exp(e2 - e1)
    w1 = pg / (1.0 + r)
    w2 = w1 * r
    x1 = gsel * PER_GROUP + i1
    x2 = gsel * PER_GROUP + i2
    ridx = lax.broadcasted_iota(jnp.int32, (LANES, 1), 0)
    gt = jnp.where(ridx == x1, w1, 0.0) + jnp.where(ridx == x2, w2, 0.0)
    gates_ref[...] = gt.T


def _outproj(oa, ob, x2, w_out, ffn_norm, wr, br, tm):
    T, D = x2.shape
    half = oa.shape[1]
    return pl.pallas_call(
        _outproj_kernel,
        grid=(T // tm,),
        in_specs=[
            pl.BlockSpec((tm, half), lambda i: (i, 0)),
            pl.BlockSpec((tm, half), lambda i: (i, 0)),
            pl.BlockSpec((tm, D), lambda i: (i, 0)),
            pl.BlockSpec(w_out.shape, lambda i: (0, 0)),
            pl.BlockSpec((1, D), lambda i: (0, 0)),
            pl.BlockSpec(wr.shape, lambda i: (0, 0)),
            pl.BlockSpec((1, LANES), lambda i: (0, 0)),
        ],
        out_specs=[
            pl.BlockSpec((tm, D), lambda i: (i, 0)),
            pl.BlockSpec((tm, D), lambda i: (i, 0)),
            pl.BlockSpec((tm, LANES), lambda i: (i, 0)),
        ],
        out_shape=[
            jax.ShapeDtypeStruct((T, D), F32),
            jax.ShapeDtypeStruct((T, D), BF16),
            jax.ShapeDtypeStruct((T, LANES), F32),
        ],
        compiler_params=_cparams(("arbitrary",)),
        name="outproj",
    )(oa, ob, x2, w_out, ffn_norm, wr, br)


def _moe_kernel(hn_ref, h_ref, gates_ref, wg_ref, wu_ref, wd_ref, fin_ref, o_ref, acc_ref):
    e = pl.program_id(1)

    @pl.when(e == 0)
    def _():
        acc_ref[...] = h_ref[...]

    lane = lax.broadcasted_iota(jnp.int32, (1, LANES), 1)
    gcol = jnp.sum(jnp.where(lane == e, gates_ref[...], 0.0), axis=-1, keepdims=True)
    t = hn_ref[...]
    a = _dot(t, wg_ref[0].astype(BF16))
    u = _dot(t, wu_ref[0].astype(BF16))
    hid = (a * jax.nn.sigmoid(a)) * u * gcol
    acc_ref[...] += _dot(hid.astype(BF16), wd_ref[0].astype(BF16))

    @pl.when(e == pl.num_programs(1) - 1)
    def _():
        y = acc_ref[...]
        ms = jnp.mean(y * y, axis=-1, keepdims=True)
        o_ref[...] = (y * lax.rsqrt(ms + NORM_EPS)) * fin_ref[...]


def _moe(hn, h, gates, w_gate, w_up, w_down, final_norm, tm):
    T, D = h.shape
    E, _, H = w_gate.shape
    return pl.pallas_call(
        _moe_kernel,
        grid=(T // tm, E),
        in_specs=[
            pl.BlockSpec((tm, D), lambda i, e: (i, 0)),
            pl.BlockSpec((tm, D), lambda i, e: (i, 0)),
            pl.BlockSpec((tm, LANES), lambda i, e: (i, 0)),
            pl.BlockSpec((1, D, H), lambda i, e: (e, 0, 0)),
            pl.BlockSpec((1, D, H), lambda i, e: (e, 0, 0)),
            pl.BlockSpec((1, H, D), lambda i, e: (e, 0, 0)),
            pl.BlockSpec((1, D), lambda i, e: (0, 0)),
        ],
        out_specs=pl.BlockSpec((tm, D), lambda i, e: (i, 0)),
        out_shape=jax.ShapeDtypeStruct((T, D), F32),
        scratch_shapes=[pltpu.VMEM((tm, D), F32)],
        compiler_params=_cparams(("arbitrary", "arbitrary")),
        name="moe",
    )(hn, h, gates, w_gate, w_up, w_down, final_norm)


def kernel(x, attn_norm, w_in, hg_lb_logits, hg_norm, fox_f_bias, fox_norm, w_out, ffn_norm,
           w_group, b_group, w_expert, b_expert, w_gate, w_up, w_down, final_norm):
    B, S, D = x.shape
    T = B * S
    assert w_in.shape[0] == 1, "single-layer block"
    hw = HG_HEADS * HG_DK
    fw = FOX_HEADS * FOX_DH
    wi = w_in[0]
    o = [0, hw, 2 * hw, 3 * hw, 4 * hw, 4 * hw + fw, 4 * hw + 2 * fw, 4 * hw + 3 * fw]
    w_all = jnp.concatenate(
        [wi[:, o[0]:o[1]], wi[:, o[2]:o[3]], wi[:, o[3]:o[4]], wi[:, o[4]:o[7]], wi[:, o[1]:o[2]],
         jnp.pad(wi[:, o[7]:], ((0, 0), (0, LANES - FOX_HEADS)))], axis=1).astype(BF16)

    x2 = x.reshape(T, D)
    tm_in = min(512, S)
    u, hgf, foxf = _inproj(x2, attn_norm.reshape(1, D), w_all, B, S, tm_in)

    cum = _foxcum(foxf.reshape(B * FOX_HEADS, S), jnp.tile(fox_f_bias[0], B).reshape(B * FOX_HEADS, 1),
                  min(256, S))
    tq = min(256, S)
    u3 = u.reshape(B, S, -1)
    o_b = _fox(u3, cum.reshape(B, FOX_HEADS // 2, 2, S // tq, tq),
               jnp.tile(fox_norm[0], 2).reshape(1, LANES), tq)
    o_a = _hgrn2(u3, hgf.reshape(B, S, hw), hg_lb_logits, hg_norm[0].reshape(1, HG_DK))

    wr = jnp.pad(jnp.concatenate([w_group[0], w_expert[0]], axis=1),
                 ((0, 0), (0, LANES - N_GROUPS - N_EXPERTS)))
    br = jnp.pad(jnp.concatenate([b_group[0], b_expert[0]]), (0, LANES - N_GROUPS - N_EXPERTS)).reshape(1, LANES)
    h, hn, gates = _outproj(o_a.reshape(T, hw), o_b.reshape(T, fw), x2, w_out[0].astype(BF16),
                            ffn_norm[0].reshape(1, D), wr, br, min(512, T))
    out = _moe(hn, h, gates, w_gate[0], w_up[0], w_down[0], final_norm.reshape(1, D), min(1024, T))
    return out.reshape(B, S, D)
```

```python
import functools

import jax
import jax.numpy as jnp
from jax import lax
from jax.experimental import pallas as pl
from jax.experimental.pallas import tpu as pltpu

F32 = jnp.float32
BF16 = jnp.bfloat16

NORM_EPS = 1e-6
NEG_INF = -1e30

LANES = 128
HG_HEADS = 4
HG_DK = 128
FOX_HEADS = 8
FOX_DH = 64
N_GROUPS = 4
PER_GROUP = 4
N_EXPERTS = 16

FOX_STRIP = 32
SUB = 16
SUPER = 128
VMEM_LIMIT = 56 * 1024 * 1024


def _cparams(sem):
    return pltpu.CompilerParams(dimension_semantics=sem, vmem_limit_bytes=VMEM_LIMIT)


def _split3(x):
    hi = x.astype(BF16)
    r1 = x - hi.astype(F32)
    mid = r1.astype(BF16)
    lo = (r1 - mid.astype(F32)).astype(BF16)
    return hi, mid, lo


def _dot(a, b):
    return jnp.dot(a, b, preferred_element_type=F32)


def _dot_nt(a, b):
    return lax.dot_general(a, b, (((1,), (1,)), ((), ())), preferred_element_type=F32)


def _dot3(m_bf16, x_f32):
    hi, mid, lo = _split3(x_f32)
    return _dot(m_bf16, hi) + _dot(m_bf16, mid) + _dot(m_bf16, lo)


def _inproj_kernel(x_ref, g_ref, w_ref, u_ref, hgf_ref, foxf_ref, xn_ref, *, n_main, col_chunk):
    x = x_ref[...]
    ms = jnp.mean(x * x, axis=-1, keepdims=True)
    xn_ref[...] = ((x * lax.rsqrt(ms + NORM_EPS)) * g_ref[...]).astype(BF16)
    for j in range(n_main // col_chunk):
        r = _dot(xn_ref[...], w_ref[:, j * col_chunk:(j + 1) * col_chunk])
        u_ref[:, j * col_chunk:(j + 1) * col_chunk] = r.astype(BF16)
    r = _dot(xn_ref[...], w_ref[:, n_main:n_main + 4 * LANES])
    hgf_ref[...] = r
    r = _dot(xn_ref[...], w_ref[:, n_main + 4 * LANES:n_main + 5 * LANES])
    foxf_ref[0] = r.T[0:FOX_HEADS, :]


def _inproj(x2, attn_norm, w_all, B, S, tm):
    T, D = x2.shape
    n_main = 6 * 512
    per_seq = S // tm
    return pl.pallas_call(
        functools.partial(_inproj_kernel, n_main=n_main, col_chunk=512),
        grid=(T // tm,),
        in_specs=[
            pl.BlockSpec((tm, D), lambda i: (i, 0)),
            pl.BlockSpec((1, D), lambda i: (0, 0)),
            pl.BlockSpec(w_all.shape, lambda i: (0, 0)),
        ],
        out_specs=[
            pl.BlockSpec((tm, n_main), lambda i: (i, 0)),
            pl.BlockSpec((tm, 4 * LANES), lambda i: (i, 0)),
            pl.BlockSpec((1, FOX_HEADS, tm), lambda i: (i // per_seq, 0, i % per_seq)),
        ],
        out_shape=[
            jax.ShapeDtypeStruct((T, n_main), BF16),
            jax.ShapeDtypeStruct((T, 4 * LANES), F32),
            jax.ShapeDtypeStruct((B, FOX_HEADS, S), F32),
        ],
        scratch_shapes=[pltpu.VMEM((tm, D), BF16)],
        compiler_params=_cparams(("arbitrary",)),
        name="inproj",
    )(x2, attn_norm, w_all)


def _foxcum_kernel(f_ref, bias_ref, cum_ref, off_ref):
    blk = f_ref.shape[1]

    @pl.when(pl.program_id(0) == 0)
    def _():
        off_ref[...] = jnp.zeros_like(off_ref)

    r_i = lax.broadcasted_iota(jnp.int32, (blk, blk), 0)
    c_i = lax.broadcasted_iota(jnp.int32, (blk, blk), 1)
    upper = (r_i <= c_i).astype(BF16)
    z = f_ref[...] + bias_ref[...]
    logf = jnp.minimum(z, 0.0) - jnp.log(1.0 + jnp.exp(-jnp.abs(z)))
    hi, mid, lo = _split3(logf)
    cum = _dot(hi, upper) + _dot(mid, upper) + _dot(lo, upper) + off_ref[...]
    cum_ref[...] = cum
    off_ref[...] = cum[:, blk - 1:blk]


def _foxcum(foxf2, bias_col, blk):
    R, S = foxf2.shape
    return pl.pallas_call(
        _foxcum_kernel,
        grid=(S // blk,),
        in_specs=[pl.BlockSpec((R, blk), lambda j: (0, j)), pl.BlockSpec((R, 1), lambda j: (0, 0))],
        out_specs=pl.BlockSpec((R, blk), lambda j: (0, j)),
        out_shape=jax.ShapeDtypeStruct((R, S), F32),
        scratch_shapes=[pltpu.VMEM((R, 1), F32)],
        compiler_params=_cparams(("arbitrary",)),
        name="foxcum",
    )(foxf2, bias_col)


def _fox_kernel(q_ref, k_ref, v_ref, cum_ref, gain_ref, o_ref, qh_scr, s_scr, p_scr, m_scr, al_scr, acc_scr, *, tq):
    qi = pl.program_id(1)
    n_pair = FOX_HEADS // 2
    lane = lax.broadcasted_iota(jnp.int32, (1, LANES), 1)
    in_h = [lane < FOX_DH, lane >= FOX_DH]
    for p in range(n_pair):
        q = q_ref[0, :, p * LANES:(p + 1) * LANES] * jnp.asarray(FOX_DH ** -0.5, BF16)
        for h in range(2):
            qh_scr[2 * p + h] = jnp.where(in_h[h], q, jnp.zeros_like(q))
    rs = FOX_STRIP
    row = lax.broadcasted_iota(jnp.int32, (rs, tq), 0)
    col = lax.broadcasted_iota(jnp.int32, (rs, tq), 1)

    m_scr[...] = jnp.full(m_scr.shape, NEG_INF, F32)
    acc_scr[...] = jnp.zeros_like(acc_scr)

    def step(kb, masked):
        r0 = pl.multiple_of(kb * tq, tq)
        for hd in range(FOX_HEADS):
            p = hd // 2
            s_scr[hd] = _dot_nt(qh_scr[hd], k_ref[0, pl.ds(r0, tq), p * LANES:(p + 1) * LANES])
        for hd in range(FOX_HEADS):
            bias = -cum_ref[0, hd // 2, hd % 2, pl.ds(kb, 1), :]
            for st in range(tq // rs):
                rows = slice(st * rs, (st + 1) * rs)
                s = s_scr[hd, rows, :] + bias
                if masked:
                    s = jnp.where(row + st * rs >= col, s, NEG_INF)
                m = m_scr[hd, rows, :]
                m_new = jnp.maximum(m, jnp.max(s, axis=-1, keepdims=True))
                p_scr[hd, rows, :] = jnp.exp(s - jnp.concatenate([m_new] * (tq // LANES), axis=1)).astype(BF16)
                al_scr[hd, rows, :] = jnp.exp(m - m_new)
                m_scr[hd, rows, :] = m_new
        for hd in range(FOX_HEADS):
            p = hd // 2
            v_blk = v_ref[0, pl.ds(r0, tq), p * LANES:(p + 1) * LANES]
            vh = jnp.where(in_h[hd % 2], v_blk, jnp.ones_like(v_blk))
            acc_scr[hd] = acc_scr[hd] * al_scr[hd] + _dot(p_scr[hd], vh)

    def body(kb, c):
        step(kb, False)
        return c

    lax.fori_loop(0, qi, body, 0)
    step(qi, True)
    for p in range(n_pair):
        a0, a1 = acc_scr[2 * p], acc_scr[2 * p + 1]
        o0 = a0 / pltpu.roll(a0, FOX_DH, 1)
        o1 = a1 / pltpu.roll(a1, FOX_DH, 1)
        o = jnp.where(in_h[0], o0, o1)
        o2 = o * o
        ms0 = jnp.sum(jnp.where(in_h[0], o2, 0.0), axis=-1, keepdims=True) * (1.0 / FOX_DH)
        ms1 = jnp.sum(jnp.where(in_h[1], o2, 0.0), axis=-1, keepdims=True) * (1.0 / FOX_DH)
        ms = jnp.where(in_h[0], ms0, ms1)
        o_ref[0, :, p * LANES:(p + 1) * LANES] = (o * lax.rsqrt(ms + NORM_EPS) * gain_ref[...]).astype(BF16)


def _fox(u3, cum5, gain2, tq):
    B, S, _ = u3.shape
    nq = S // tq
    fw = FOX_HEADS * FOX_DH
    qc, kc, vc = 3, 4, 5
    return pl.pallas_call(
        functools.partial(_fox_kernel, tq=tq),
        grid=(B, nq),
        in_specs=[
            pl.BlockSpec((1, tq, fw), lambda b, i: (b, i, qc)),
            pl.BlockSpec((1, S, fw), lambda b, i: (b, 0, kc)),
            pl.BlockSpec((1, S, fw), lambda b, i: (b, 0, vc)),
            pl.BlockSpec((1, FOX_HEADS // 2, 2, nq, tq), lambda b, i: (b, 0, 0, 0, 0)),
            pl.BlockSpec((1, LANES), lambda b, i: (0, 0)),
        ],
        out_specs=pl.BlockSpec((1, tq, fw), lambda b, i: (b, i, 0)),
        out_shape=jax.ShapeDtypeStruct((B, S, fw), BF16),
        scratch_shapes=[
            pltpu.VMEM((FOX_HEADS, tq, LANES), BF16),
            pltpu.VMEM((FOX_HEADS, tq, tq), F32),
            pltpu.VMEM((FOX_HEADS, tq, tq), BF16),
            pltpu.VMEM((FOX_HEADS, tq, LANES), F32),
            pltpu.VMEM((FOX_HEADS, tq, LANES), F32),
            pltpu.VMEM((FOX_HEADS, tq, LANES), F32),
        ],
        compiler_params=_cparams(("arbitrary", "arbitrary")),
        name="fox",
    )(u3, u3, u3, cum5, gain2)


def _hgrn2_kernel(q_ref, i_ref, g_ref, f_ref, lbl_ref, gain_ref, o_ref, b_scr, q_scr, k_scr, st_scr):
    S = q_ref.shape[1]
    n_sub = SUPER // SUB
    a = lbl_ref[...]
    am = jnp.max(a, axis=0, keepdims=True)
    ea = jnp.exp(a - am)
    lb = ea[0:1, :] / (ea[0:1, :] + ea[1:2, :])

    r_i = lax.broadcasted_iota(jnp.int32, (SUPER, SUPER), 0)
    c_i = lax.broadcasted_iota(jnp.int32, (SUPER, SUPER), 1)
    same = (r_i // SUB) == (c_i // SUB)
    tri = (same & (c_i <= r_i)).astype(BF16)
    blk1 = same.astype(BF16)
    ones = jnp.ones((LANES, LANES), BF16)
    lane = lax.broadcasted_iota(jnp.int32, (1, LANES), 1)
    trow = lax.broadcasted_iota(jnp.int32, (8, LANES), 0)

    st_scr[...] = jnp.zeros_like(st_scr)

    def superblock(sb, _):
        r0 = pl.multiple_of(sb * SUPER, SUPER)
        qr = q_ref[0, pl.ds(r0, SUPER), :].astype(F32)
        q = qr * jax.nn.sigmoid(qr)
        f = lb + (1.0 - lb) * jax.nn.sigmoid(f_ref[0, pl.ds(r0, SUPER), :])
        logf = jnp.log(f)
        k = 1.0 - f
        v = i_ref[0, pl.ds(r0, SUPER), :]
        b = _dot3(tri, logf)
        btot = _dot3(blk1, logf)
        qt = (q * jnp.exp(b)).astype(BF16)
        kh = (k * jnp.exp(btot - b)).astype(BF16)
        gam = jnp.exp(btot)
        b_scr[...] = b
        q_scr[...] = q
        k_scr[...] = k
        vt = v.astype(F32).T

        st = st_scr[...]
        o_inter = []
        for c in range(n_sub):
            o_inter.append(_dot_nt(qt[c * SUB:(c + 1) * SUB, :], st.astype(BF16)))
            vtm = jnp.where((lane // SUB) == c, vt, 0.0).astype(BF16)
            st = gam[c * SUB:c * SUB + 1, :] * st + _dot(vtm, kh)
        st_scr[...] = st

        a_rows = []
        for c in range(n_sub):
            base = c * SUB
            tiles = []
            lo_half = []
            for j in range(SUB):
                bs = b_scr[base + j:base + j + 1, :]
                ks = k_scr[base + j:base + j + 1, :]
                halves = []
                for hf in range(2):
                    if hf == 0 and j >= 8:
                        continue
                    t0 = base + 8 * hf
                    p = (q_scr[t0:t0 + 8, :] * jnp.exp(b_scr[t0:t0 + 8, :] - bs)) * ks
                    if j // 8 == hf:
                        p = jnp.where(trow >= (j % 8), p, 0.0)
                    halves.append(p)
                if j < 8:
                    tiles.append(jnp.concatenate(halves, axis=0))
                else:
                    lo_half.append(halves[0])
            for m in range(4):
                tiles.append(jnp.concatenate([lo_half[2 * m], lo_half[2 * m + 1]], axis=0))
            pc = jnp.concatenate(tiles, axis=0).astype(BF16)
            rc = _dot(pc, ones)
            a_top = jnp.zeros((8, LANES), F32)
            a_bot = jnp.zeros((8, LANES), F32)
            for j in range(8):
                sel = lane == (base + j)
                a_top = jnp.where(sel, rc[16 * j:16 * j + 8, :], a_top)
                a_bot = jnp.where(sel, rc[16 * j + 8:16 * j + 16, :], a_bot)
            for j in range(8, SUB):
                sel = lane == (base + j)
                a_bot = jnp.where(sel, rc[128 + 8 * (j - 8):128 + 8 * (j - 8) + 8, :], a_bot)
            a_rows += [a_top, a_bot]
        a_blk = jnp.concatenate(a_rows, axis=0).astype(BF16)
        o = jnp.concatenate(o_inter, axis=0) + _dot(a_blk, v)
        ms = jnp.mean(o * o, axis=-1, keepdims=True)
        o = o * lax.rsqrt(ms + NORM_EPS) * gain_ref[...]
        gr = g_ref[0, pl.ds(r0, SUPER), :].astype(F32)
        o_ref[0, pl.ds(r0, SUPER), :] = (o * (gr * jax.nn.sigmoid(gr))).astype(BF16)
        return 0

    lax.fori_loop(0, S // SUPER, superblock, 0)


def _hgrn2(u3, hgf3, lb_logits, gain):
    B, S, _ = u3.shape
    return pl.pallas_call(
        _hgrn2_kernel,
        grid=(B, HG_HEADS),
        in_specs=[
            pl.BlockSpec((1, S, LANES), lambda b, h: (b, 0, h)),
            pl.BlockSpec((1, S, LANES), lambda b, h: (b, 0, 4 + h)),
            pl.BlockSpec((1, S, LANES), lambda b, h: (b, 0, 8 + h)),
            pl.BlockSpec((1, S, LANES), lambda b, h: (b, 0, h)),
            pl.BlockSpec((2, LANES), lambda b, h: (0, h)),
            pl.BlockSpec((1, LANES), lambda b, h: (0, 0)),
        ],
        out_specs=pl.BlockSpec((1, S, LANES), lambda b, h: (b, 0, h)),
        out_shape=jax.ShapeDtypeStruct((B, S, HG_HEADS * HG_DK), BF16),
        scratch_shapes=[pltpu.VMEM((SUPER, LANES), F32)] * 4,
        compiler_params=_cparams(("arbitrary", "arbitrary")),
        name="hgrn2",
    )(u3, u3, u3, hgf3, lb_logits, gain)


def _outproj_kernel(oa_ref, ob_ref, x_ref, wo_ref, g_ref, wr_ref, br_ref, h_ref, hn_ref, gates_ref):
    half = oa_ref.shape[1]
    h = x_ref[...] + _dot(oa_ref[...], wo_ref[0:half, :]) + _dot(ob_ref[...], wo_ref[half:2 * half, :])
    h_ref[...] = h
    ms = jnp.mean(h * h, axis=-1, keepdims=True)
    hn = (h * lax.rsqrt(ms + NORM_EPS)) * g_ref[...]
    hn_ref[...] = hn.astype(BF16)
    hn_hi = hn.astype(BF16)
    hn_lo = (hn - hn_hi.astype(F32)).astype(BF16)
    wr = wr_ref[...]
    wr_hi = wr.astype(BF16)
    wr_lo = (wr - wr_hi.astype(F32)).astype(BF16)
    logits = _dot(hn_hi, wr_hi) + _dot(hn_lo, wr_hi) + _dot(hn_hi, wr_lo) + br_ref[...]
    lt = logits.T
    g = [lt[i:i + 1, :] for i in range(N_GROUPS)]
    gm = jnp.maximum(jnp.maximum(g[0], g[1]), jnp.maximum(g[2], g[3]))
    gsel = jnp.where(g[0] == gm, 0, jnp.where(g[1] == gm, 1, jnp.where(g[2] == gm, 2, 3)))
    pg = 1.0 / (jnp.exp(g[0] - gm) + jnp.exp(g[1] - gm) + jnp.exp(g[2] - gm) + jnp.exp(g[3] - gm))
    e = []
    for i in range(PER_GROUP):
        rows = [lt[N_GROUPS + PER_GROUP * gg + i:N_GROUPS + PER_GROUP * gg + i + 1, :] for gg in range(N_GROUPS)]
        e.append(jnp.where(gsel == 0, rows[0], jnp.where(gsel == 1, rows[1], jnp.where(gsel == 2, rows[2], rows[3]))))
    e1 = jnp.maximum(jnp.maximum(e[0], e[1]), jnp.maximum(e[2], e[3]))
    i1 = jnp.where(e[0] == e1, 0, jnp.where(e[1] == e1, 1, jnp.where(e[2] == e1, 2, 3)))
    ex = [jnp.where(i1 == i, -jnp.inf, e[i]) for i in range(PER_GROUP)]
    e2 = jnp.maximum(jnp.maximum(ex[0], ex[1]), jnp.maximum(ex[2], ex[3]))
    i2 = jnp.where(ex[0] == e2, 0, jnp.where(ex[1] == e2, 1, jnp.where(ex[2] == e2, 2, 3)))
    r = jnp.exp(e2 - e1)
    w1 = pg / (1.0 + r)
    w2 = w1 * r
    x1 = gsel * PER_GROUP + i1
    x2 = gsel * PER_GROUP + i2
    ridx = lax.broadcasted_iota(jnp.int32, (LANES, 1), 0)
    gt = jnp.where(ridx == x1, w1, 0.0) + jnp.where(ridx == x2, w2, 0.0)
    gates_ref[...] = gt.T


def _outproj(oa, ob, x2, w_out, ffn_norm, wr, br, tm):
    T, D = x2.shape
    half = oa.shape[1]
    return pl.pallas_call(
        _outproj_kernel,
        grid=(T // tm,),
        in_specs=[
            pl.BlockSpec((tm, half), lambda i: (i, 0)),
            pl.BlockSpec((tm, half), lambda i: (i, 0)),
            pl.BlockSpec((tm, D), lambda i: (i, 0)),
            pl.BlockSpec(w_out.shape, lambda i: (0, 0)),
            pl.BlockSpec((1, D), lambda i: (0, 0)),
            pl.BlockSpec(wr.shape, lambda i: (0, 0)),
            pl.BlockSpec((1, LANES), lambda i: (0, 0)),
        ],
        out_specs=[
            pl.BlockSpec((tm, D), lambda i: (i, 0)),
            pl.BlockSpec((tm, D), lambda i: (i, 0)),
            pl.BlockSpec((tm, LANES), lambda i: (i, 0)),
        ],
        out_shape=[
            jax.ShapeDtypeStruct((T, D), F32),
            jax.ShapeDtypeStruct((T, D), BF16),
            jax.ShapeDtypeStruct((T, LANES), F32),
        ],
        compiler_params=_cparams(("arbitrary",)),
        name="outproj",
    )(oa, ob, x2, w_out, ffn_norm, wr, br)


def _moe_kernel(hn_ref, h_ref, gates_ref, wg_ref, wu_ref, wd_ref, fin_ref, o_ref, acc_ref):
    e = pl.program_id(1)

    @pl.when(e == 0)
    def _():
        acc_ref[...] = h_ref[...]

    lane = lax.broadcasted_iota(jnp.int32, (1, LANES), 1)
    gcol = jnp.sum(jnp.where(lane == e, gates_ref[...], 0.0), axis=-1, keepdims=True)
    t = hn_ref[...]
    a = _dot(t, wg_ref[0].astype(BF16))
    u = _dot(t, wu_ref[0].astype(BF16))
    hid = (a * jax.nn.sigmoid(a)) * u * gcol
    acc_ref[...] += _dot(hid.astype(BF16), wd_ref[0].astype(BF16))

    @pl.when(e == pl.num_programs(1) - 1)
    def _():
        y = acc_ref[...]
        ms = jnp.mean(y * y, axis=-1, keepdims=True)
        o_ref[...] = (y * lax.rsqrt(ms + NORM_EPS)) * fin_ref[...]


def _moe(hn, h, gates, w_gate, w_up, w_down, final_norm, tm):
    T, D = h.shape
    E, _, H = w_gate.shape
    return pl.pallas_call(
        _moe_kernel,
        grid=(T // tm, E),
        in_specs=[
            pl.BlockSpec((tm, D), lambda i, e: (i, 0)),
            pl.BlockSpec((tm, D), lambda i, e: (i, 0)),
            pl.BlockSpec((tm, LANES), lambda i, e: (i, 0)),
            pl.BlockSpec((1, D, H), lambda i, e: (e, 0, 0)),
            pl.BlockSpec((1, D, H), lambda i, e: (e, 0, 0)),
            pl.BlockSpec((1, H, D), lambda i, e: (e, 0, 0)),
            pl.BlockSpec((1, D), lambda i, e: (0, 0)),
        ],
        out_specs=pl.BlockSpec((tm, D), lambda i, e: (i, 0)),
        out_shape=jax.ShapeDtypeStruct((T, D), F32),
        scratch_shapes=[pltpu.VMEM((tm, D), F32)],
        compiler_params=_cparams(("arbitrary", "arbitrary")),
        name="moe",
    )(hn, h, gates, w_gate, w_up, w_down, final_norm)


def kernel(x, attn_norm, w_in, hg_lb_logits, hg_norm, fox_f_bias, fox_norm, w_out, ffn_norm,
           w_group, b_group, w_expert, b_expert, w_gate, w_up, w_down, final_norm):
    B, S, D = x.shape
    T = B * S
    assert w_in.shape[0] == 1, "single-layer block"
    hw = HG_HEADS * HG_DK
    fw = FOX_HEADS * FOX_DH
    wi = w_in[0]
    o = [0, hw, 2 * hw, 3 * hw, 4 * hw, 4 * hw + fw, 4 * hw + 2 * fw, 4 * hw + 3 * fw]
    w_all = jnp.concatenate(
        [wi[:, o[0]:o[1]], wi[:, o[2]:o[3]], wi[:, o[3]:o[4]], wi[:, o[4]:o[7]], wi[:, o[1]:o[2]],
         jnp.pad(wi[:, o[7]:], ((0, 0), (0, LANES - FOX_HEADS)))], axis=1).astype(BF16)

    x2 = x.reshape(T, D)
    tm_in = min(512, S)
    u, hgf, foxf = _inproj(x2, attn_norm.reshape(1, D), w_all, B, S, tm_in)

    cum = _foxcum(foxf.reshape(B * FOX_HEADS, S), jnp.tile(fox_f_bias[0], B).reshape(B * FOX_HEADS, 1),
                  min(256, S))
    tq = min(256, S)
    u3 = u.reshape(B, S, -1)
    o_b = _fox(u3, cum.reshape(B, FOX_HEADS // 2, 2, S // tq, tq),
               jnp.tile(fox_norm[0], 2).reshape(1, LANES), tq)
    o_a = _hgrn2(u3, hgf.reshape(B, S, hw), hg_lb_logits, hg_norm[0].reshape(1, HG_DK))

    wr = jnp.pad(jnp.concatenate([w_group[0], w_expert[0]], axis=1),
                 ((0, 0), (0, LANES - N_GROUPS - N_EXPERTS)))
    br = jnp.pad(jnp.concatenate([b_group[0], b_expert[0]]), (0, LANES - N_GROUPS - N_EXPERTS)).reshape(1, LANES)
    h, hn, gates = _outproj(o_a.reshape(T, hw), o_b.reshape(T, fw), x2, w_out[0].astype(BF16),
                            ffn_norm[0].reshape(1, D), wr, br, min(512, T))
    out = _moe(hn, h, gates, w_gate[0], w_up[0], w_down[0], final_norm.reshape(1, D), min(1024, T))
    return out.reshape(B, S, D)
```

```python
import functools

import jax
import jax.numpy as jnp
from jax import lax
from jax.experimental import pallas as pl
from jax.experimental.pallas import tpu as pltpu

F32 = jnp.float32
BF16 = jnp.bfloat16

NORM_EPS = 1e-6
NEG_INF = -1e30

LANES = 128
HG_HEADS = 4
HG_DK = 128
FOX_HEADS = 8
FOX_DH = 64
N_GROUPS = 4
PER_GROUP = 4
N_EXPERTS = 16

FOX_STRIP = 32
SUB = 16
SUPER = 128
VMEM_LIMIT = 56 * 1024 * 1024


def _cparams(sem):
    return pltpu.CompilerParams(dimension_semantics=sem, vmem_limit_bytes=VMEM_LIMIT)


def _split3(x):
    hi = x.astype(BF16)
    r1 = x - hi.astype(F32)
    mid = r1.astype(BF16)
    lo = (r1 - mid.astype(F32)).astype(BF16)
    return hi, mid, lo


def _dot(a, b):
    return jnp.dot(a, b, preferred_element_type=F32)


def _dot_nt(a, b):
    return lax.dot_general(a, b, (((1,), (1,)), ((), ())), preferred_element_type=F32)


def _dot3(m_bf16, x_f32):
    hi, mid, lo = _split3(x_f32)
    return _dot(m_bf16, hi) + _dot(m_bf16, mid) + _dot(m_bf16, lo)


def _inproj_kernel(x_ref, g_ref, w_ref, u_ref, hgf_ref, foxf_ref, xn_ref, *, n_main, col_chunk):
    x = x_ref[...]
    ms = jnp.mean(x * x, axis=-1, keepdims=True)
    xn_ref[...] = ((x * lax.rsqrt(ms + NORM_EPS)) * g_ref[...]).astype(BF16)
    for j in range(n_main // col_chunk):
        r = _dot(xn_ref[...], w_ref[:, j * col_chunk:(j + 1) * col_chunk])
        u_ref[:, j * col_chunk:(j + 1) * col_chunk] = r.astype(BF16)
    r = _dot(xn_ref[...], w_ref[:, n_main:n_main + 4 * LANES])
    hgf_ref[...] = r
    r = _dot(xn_ref[...], w_ref[:, n_main + 4 * LANES:n_main + 5 * LANES])
    foxf_ref[0] = r.T[0:FOX_HEADS, :]


def _inproj(x2, attn_norm, w_all, B, S, tm):
    T, D = x2.shape
    n_main = 6 * 512
    per_seq = S // tm
    return pl.pallas_call(
        functools.partial(_inproj_kernel, n_main=n_main, col_chunk=512),
        grid=(T // tm,),
        in_specs=[
            pl.BlockSpec((tm, D), lambda i: (i, 0)),
            pl.BlockSpec((1, D), lambda i: (0, 0)),
            pl.BlockSpec(w_all.shape, lambda i: (0, 0)),
        ],
        out_specs=[
            pl.BlockSpec((tm, n_main), lambda i: (i, 0)),
            pl.BlockSpec((tm, 4 * LANES), lambda i: (i, 0)),
            pl.BlockSpec((1, FOX_HEADS, tm), lambda i: (i // per_seq, 0, i % per_seq)),
        ],
        out_shape=[
            jax.ShapeDtypeStruct((T, n_main), BF16),
            jax.ShapeDtypeStruct((T, 4 * LANES), F32),
            jax.ShapeDtypeStruct((B, FOX_HEADS, S), F32),
        ],
        scratch_shapes=[pltpu.VMEM((tm, D), BF16)],
        compiler_params=_cparams(("arbitrary",)),
        name="inproj",
    )(x2, attn_norm, w_all)


def _foxcum_kernel(f_ref, bias_ref, cum_ref, off_ref):
    blk = f_ref.shape[1]

    @pl.when(pl.program_id(0) == 0)
    def _():
        off_ref[...] = jnp.zeros_like(off_ref)

    r_i = lax.broadcasted_iota(jnp.int32, (blk, blk), 0)
    c_i = lax.broadcasted_iota(jnp.int32, (blk, blk), 1)
    upper = (r_i <= c_i).astype(BF16)
    z = f_ref[...] + bias_ref[...]
    logf = jnp.minimum(z, 0.0) - jnp.log(1.0 + jnp.exp(-jnp.abs(z)))
    hi, mid, lo = _split3(logf)
    cum = _dot(hi, upper) + _dot(mid, upper) + _dot(lo, upper) + off_ref[...]
    cum_ref[...] = cum
    off_ref[...] = cum[:, blk - 1:blk]


def _foxcum(foxf2, bias_col, blk):
    R, S = foxf2.shape
    return pl.pallas_call(
        _foxcum_kernel,
        grid=(S // blk,),
        in_specs=[pl.BlockSpec((R, blk), lambda j: (0, j)), pl.BlockSpec((R, 1), lambda j: (0, 0))],
        out_specs=pl.BlockSpec((R, blk), lambda j: (0, j)),
        out_shape=jax.ShapeDtypeStruct((R, S), F32),
        scratch_shapes=[pltpu.VMEM((R, 1), F32)],
        compiler_params=_cparams(("arbitrary",)),
        name="foxcum",
    )(foxf2, bias_col)


def _fox_kernel(q_ref, k_ref, v_ref, cum_ref, gain_ref, o_ref, qh_scr, s_scr, p_scr, m_scr, al_scr, acc_scr, *, tq):
    qi = pl.program_id(1)
    n_pair = FOX_HEADS // 2
    lane = lax.broadcasted_iota(jnp.int32, (1, LANES), 1)
    in_h = [lane < FOX_DH, lane >= FOX_DH]
    for p in range(n_pair):
        q = q_ref[0, :, p * LANES:(p + 1) * LANES] * jnp.asarray(FOX_DH ** -0.5, BF16)
        for h in range(2):
            qh_scr[2 * p + h] = jnp.where(in_h[h], q, jnp.zeros_like(q))
    rs = FOX_STRIP
    row = lax.broadcasted_iota(jnp.int32, (rs, tq), 0)
    col = lax.broadcasted_iota(jnp.int32, (rs, tq), 1)

    m_scr[...] = jnp.full(m_scr.shape, NEG_INF, F32)
    acc_scr[...] = jnp.zeros_like(acc_scr)

    def step(kb, masked):
        r0 = pl.multiple_of(kb * tq, tq)
        for hd in range(FOX_HEADS):
            p = hd // 2
            s_scr[hd] = _dot_nt(qh_scr[hd], k_ref[0, pl.ds(r0, tq), p * LANES:(p + 1) * LANES])
        for hd in range(FOX_HEADS):
            bias = -cum_ref[0, hd // 2, hd % 2, pl.ds(kb, 1), :]
            for st in range(tq // rs):
                rows = slice(st * rs, (st + 1) * rs)
                s = s_scr[hd, rows, :] + bias
                if masked:
                    s = jnp.where(row + st * rs >= col, s, NEG_INF)
                m = m_scr[hd, rows, :]
                m_new = jnp.maximum(m, jnp.max(s, axis=-1, keepdims=True))
                p_scr[hd, rows, :] = jnp.exp(s - jnp.concatenate([m_new] * (tq // LANES), axis=1)).astype(BF16)
                al_scr[hd, rows, :] = jnp.exp(m - m_new)
                m_scr[hd, rows, :] = m_new
        for hd in range(FOX_HEADS):
            p = hd // 2
            v_blk = v_ref[0, pl.ds(r0, tq), p * LANES:(p + 1) * LANES]
            vh = jnp.where(in_h[hd % 2], v_blk, jnp.ones_like(v_blk))
            acc_scr[hd] = acc_scr[hd] * al_scr[hd] + _dot(p_scr[hd], vh)

    def body(kb, c):
        step(kb, False)
        return c

    lax.fori_loop(0, qi, body, 0)
    step(qi, True)
    for p in range(n_pair):
        a0, a1 = acc_scr[2 * p], acc_scr[2 * p + 1]
        o0 = a0 / pltpu.roll(a0, FOX_DH, 1)
        o1 = a1 / pltpu.roll(a1, FOX_DH, 1)
        o = jnp.where(in_h[0], o0, o1)
        o2 = o * o
        ms0 = jnp.sum(jnp.where(in_h[0], o2, 0.0), axis=-1, keepdims=True) * (1.0 / FOX_DH)
        ms1 = jnp.sum(jnp.where(in_h[1], o2, 0.0), axis=-1, keepdims=True) * (1.0 / FOX_DH)
        ms = jnp.where(in_h[0], ms0, ms1)
        o_ref[0, :, p * LANES:(p + 1) * LANES] = (o * lax.rsqrt(ms + NORM_EPS) * gain_ref[...]).astype(BF16)


def _fox(u3, cum5, gain2, tq):
    B, S, _ = u3.shape
    nq = S // tq
    fw = FOX_HEADS * FOX_DH
    qc, kc, vc = 3, 4, 5
    return pl.pallas_call(
        functools.partial(_fox_kernel, tq=tq),
        grid=(B, nq),
        in_specs=[
            pl.BlockSpec((1, tq, fw), lambda b, i: (b, i, qc)),
            pl.BlockSpec((1, S, fw), lambda b, i: (b, 0, kc)),
            pl.BlockSpec((1, S, fw), lambda b, i: (b, 0, vc)),
            pl.BlockSpec((1, FOX_HEADS // 2, 2, nq, tq), lambda b, i: (b, 0, 0, 0, 0)),
            pl.BlockSpec((1, LANES), lambda b, i: (0, 0)),
        ],
        out_specs=pl.BlockSpec((1, tq, fw), lambda b, i: (b, i, 0)),
        out_shape=jax.ShapeDtypeStruct((B, S, fw), BF16),
        scratch_shapes=[
            pltpu.VMEM((FOX_HEADS, tq, LANES), BF16),
            pltpu.VMEM((FOX_HEADS, tq, tq), F32),
            pltpu.VMEM((FOX_HEADS, tq, tq), BF16),
            pltpu.VMEM((FOX_HEADS, tq, LANES), F32),
            pltpu.VMEM((FOX_HEADS, tq, LANES), F32),
            pltpu.VMEM((FOX_HEADS, tq, LANES), F32),
        ],
        compiler_params=_cparams(("arbitrary", "arbitrary")),
        name="fox",
    )(u3, u3, u3, cum5, gain2)


def _hgrn2_kernel(q_ref, i_ref, g_ref, f_ref, lbl_ref, gain_ref, o_ref, b_scr, q_scr, k_scr, st_scr):
    S = q_ref.shape[1]
    n_sub = SUPER // SUB
    a = lbl_ref[...]
    am = jnp.max(a, axis=0, keepdims=True)
    ea = jnp.exp(a - am)
    lb = ea[0:1, :] / (ea[0:1, :] + ea[1:2, :])

    r_i = lax.broadcasted_iota(jnp.int32, (SUPER, SUPER), 0)
    c_i = lax.broadcasted_iota(jnp.int32, (SUPER, SUPER), 1)
    same = (r_i // SUB) == (c_i // SUB)
    tri = (same & (c_i <= r_i)).astype(BF16)
    blk1 = same.astype(BF16)
    ones = jnp.ones((LANES, LANES), BF16)
    lane = lax.broadcasted_iota(jnp.int32, (1, LANES), 1)
    trow = lax.broadcasted_iota(jnp.int32, (8, LANES), 0)

    st_scr[...] = jnp.zeros_like(st_scr)

    def superblock(sb, _):
        r0 = pl.multiple_of(sb * SUPER, SUPER)
        qr = q_ref[0, pl.ds(r0, SUPER), :].astype(F32)
        q = qr * jax.nn.sigmoid(qr)
        f = lb + (1.0 - lb) * jax.nn.sigmoid(f_ref[0, pl.ds(r0, SUPER), :])
        logf = jnp.log(f)
        k = 1.0 - f
        v = i_ref[0, pl.ds(r0, SUPER), :]
        b = _dot3(tri, logf)
        btot = _dot3(blk1, logf)
        qt = (q * jnp.exp(b)).astype(BF16)
        kh = (k * jnp.exp(btot - b)).astype(BF16)
        gam = jnp.exp(btot)
        b_scr[...] = b
        q_scr[...] = q
        k_scr[...] = k
        vt = v.astype(F32).T

        st = st_scr[...]
        o_inter = []
        for c in range(n_sub):
            o_inter.append(_dot_nt(qt[c * SUB:(c + 1) * SUB, :], st.astype(BF16)))
            vtm = jnp.where((lane // SUB) == c, vt, 0.0).astype(BF16)
            st = gam[c * SUB:c * SUB + 1, :] * st + _dot(vtm, kh)
        st_scr[...] = st

        a_rows = []
        for c in range(n_sub):
            base = c * SUB
            tiles = []
            lo_half = []
            for j in range(SUB):
                bs = b_scr[base + j:base + j + 1, :]
                ks = k_scr[base + j:base + j + 1, :]
                halves = []
                for hf in range(2):
                    if hf == 0 and j >= 8:
                        continue
                    t0 = base + 8 * hf
                    p = (q_scr[t0:t0 + 8, :] * jnp.exp(b_scr[t0:t0 + 8, :] - bs)) * ks
                    if j // 8 == hf:
                        p = jnp.where(trow >= (j % 8), p, 0.0)
                    halves.append(p)
                if j < 8:
                    tiles.append(jnp.concatenate(halves, axis=0))
                else:
                    lo_half.append(halves[0])
            for m in range(4):
                tiles.append(jnp.concatenate([lo_half[2 * m], lo_half[2 * m + 1]], axis=0))
            pc = jnp.concatenate(tiles, axis=0).astype(BF16)
            rc = _dot(pc, ones)
            a_top = jnp.zeros((8, LANES), F32)
            a_bot = jnp.zeros((8, LANES), F32)
            for j in range(8):
                sel = lane == (base + j)
                a_top = jnp.where(sel, rc[16 * j:16 * j + 8, :], a_top)
                a_bot = jnp.where(sel, rc[16 * j + 8:16 * j + 16, :], a_bot)
            for j in range(8, SUB):
                sel = lane == (base + j)
                a_bot = jnp.where(sel, rc[128 + 8 * (j - 8):128 + 8 * (j - 8) + 8, :], a_bot)
            a_rows += [a_top, a_bot]
        a_blk = jnp.concatenate(a_rows, axis=0).astype(BF16)
        o = jnp.concatenate(o_inter, axis=0) + _dot(a_blk, v)
        ms = jnp.mean(o * o, axis=-1, keepdims=True)
        o = o * lax.rsqrt(ms + NORM_EPS) * gain_ref[...]
        gr = g_ref[0, pl.ds(r0, SUPER), :].astype(F32)
        o_ref[0, pl.ds(r0, SUPER), :] = (o * (gr * jax.nn.sigmoid(gr))).astype(BF16)
        return 0

    lax.fori_loop(0, S // SUPER, superblock, 0)


def _hgrn2(u3, hgf3, lb_logits, gain):
    B, S, _ = u3.shape
    return pl.pallas_call(
        _hgrn2_kernel,
        grid=(B, HG_HEADS),
        in_specs=[
            pl.BlockSpec((1, S, LANES), lambda b, h: (b, 0, h)),
            pl.BlockSpec((1, S, LANES), lambda b, h: (b, 0, 4 + h)),
            pl.BlockSpec((1, S, LANES), lambda b, h: (b, 0, 8 + h)),
            pl.BlockSpec((1, S, LANES), lambda b, h: (b, 0, h)),
            pl.BlockSpec((2, LANES), lambda b, h: (0, h)),
            pl.BlockSpec((1, LANES), lambda b, h: (0, 0)),
        ],
        out_specs=pl.BlockSpec((1, S, LANES), lambda b, h: (b, 0, h)),
        out_shape=jax.ShapeDtypeStruct((B, S, HG_HEADS * HG_DK), BF16),
        scratch_shapes=[pltpu.VMEM((SUPER, LANES), F32)] * 4,
        compiler_params=_cparams(("arbitrary", "arbitrary")),
        name="hgrn2",
    )(u3, u3, u3, hgf3, lb_logits, gain)


def _hgrn2w_kernel(q_ref, i_ref, g_ref, f_ref, lbl_ref, gain_ref, o_ref, b_scr, q_scr, c_scr, st_scr):
    S, W = q_ref.shape[1], q_ref.shape[2]
    nh = W // LANES
    n_sub = SUPER // SUB
    hsl = [slice(h * LANES, (h + 1) * LANES) for h in range(nh)]
    a = lbl_ref[...]
    am = jnp.max(a, axis=0, keepdims=True)
    ea = jnp.exp(a - am)
    lb = ea[0:1, :] / (ea[0:1, :] + ea[1:2, :])

    r_i = lax.broadcasted_iota(jnp.int32, (SUPER, SUPER), 0)
    c_i = lax.broadcasted_iota(jnp.int32, (SUPER, SUPER), 1)
    same = (r_i // SUB) == (c_i // SUB)
    tri = (same & (c_i <= r_i)).astype(BF16)
    blk1 = same.astype(BF16)
    ones = jnp.ones((LANES, LANES), BF16)
    lane = lax.broadcasted_iota(jnp.int32, (1, LANES), 1)
    trow = lax.broadcasted_iota(jnp.int32, (8, W), 0)

    st_scr[...] = jnp.zeros_like(st_scr)

    def superblock(sb, _):
        r0 = pl.multiple_of(sb * SUPER, SUPER)
        qr = q_ref[0, pl.ds(r0, SUPER), :].astype(F32)
        q = qr * jax.nn.sigmoid(qr)
        f = lb + (1.0 - lb) * jax.nn.sigmoid(f_ref[0, pl.ds(r0, SUPER), :])
        lf2 = jnp.log2(f)
        b2 = _dot3(tri, lf2)
        bt2 = _dot3(blk1, lf2)
        c2 = b2 - jnp.log2(1.0 - f)
        qt = (q * jnp.exp2(b2)).astype(BF16)
        kh = jnp.exp2(bt2 - c2).astype(BF16)
        gam = jnp.exp2(bt2)
        b_scr[...] = b2
        q_scr[...] = q
        c_scr[...] = c2
        v = i_ref[0, pl.ds(r0, SUPER), :]
        vt = [v[:, hsl[h]].astype(F32).T for h in range(nh)]

        st = [st_scr[h] for h in range(nh)]
        o_inter = [[] for _ in range(nh)]
        for c in range(n_sub):
            rows = slice(c * SUB, (c + 1) * SUB)
            for h in range(nh):
                o_inter[h].append(_dot_nt(qt[rows, hsl[h]], st[h].astype(BF16)))
                dst = lax.dot_general(v[rows, hsl[h]], kh[rows, hsl[h]], (((0,), (0,)), ((), ())),
                                      preferred_element_type=F32)
                st[h] = gam[c * SUB:c * SUB + 1, hsl[h]] * st[h] + dst
        for h in range(nh):
            st_scr[h] = st[h]

        a_rows = [[] for _ in range(nh)]
        for c in range(n_sub):
            base = c * SUB
            tiles = []
            lo_half = []
            for j in range(SUB):
                cs = c_scr[base + j:base + j + 1, :]
                halves = []
                for hf in range(2):
                    if hf == 0 and j >= 8:
                        continue
                    t0 = base + 8 * hf
                    p = q_scr[t0:t0 + 8, :] * jnp.exp2(b_scr[t0:t0 + 8, :] - cs)
                    if j // 8 == hf:
                        p = jnp.where(trow >= (j % 8), p, 0.0)
                    halves.append(p)
                if j < 8:
                    tiles.append(jnp.concatenate(halves, axis=0))
                else:
                    lo_half.append(halves[0])
            for m in range(4):
                tiles.append(jnp.concatenate([lo_half[2 * m], lo_half[2 * m + 1]], axis=0))
            pc = jnp.concatenate(tiles, axis=0).astype(BF16)
            for h in range(nh):
                rc = _dot(pc[:, hsl[h]], ones)
                a_top = jnp.zeros((8, LANES), F32)
                a_bot = jnp.zeros((8, LANES), F32)
                for j in range(8):
                    sel = lane == (base + j)
                    a_top = jnp.where(sel, rc[16 * j:16 * j + 8, :], a_top)
                    a_bot = jnp.where(sel, rc[16 * j + 8:16 * j + 16, :], a_bot)
                for j in range(8, SUB):
                    sel = lane == (base + j)
                    a_bot = jnp.where(sel, rc[128 + 8 * (j - 8):128 + 8 * (j - 8) + 8, :], a_bot)
                a_rows[h] += [a_top, a_bot]

        outs = []
        for h in range(nh):
            a_blk = jnp.concatenate(a_rows[h], axis=0).astype(BF16)
            o = jnp.concatenate(o_inter[h], axis=0) + _dot(a_blk, v[:, hsl[h]])
            ms = jnp.mean(o * o, axis=-1, keepdims=True)
            outs.append(o * lax.rsqrt(ms + NORM_EPS))
        o = jnp.concatenate(outs, axis=1) * gain_ref[...]
        gr = g_ref[0, pl.ds(r0, SUPER), :].astype(F32)
        o_ref[0, pl.ds(r0, SUPER), :] = (o * (gr * jax.nn.sigmoid(gr))).astype(BF16)
        return 0

    lax.fori_loop(0, S // SUPER, superblock, 0)


def _hgrn2w(u3, hgf3, lb_logits, gain_w):
    B, S, _ = u3.shape
    W = HG_HEADS * HG_DK
    return pl.pallas_call(
        _hgrn2w_kernel,
        grid=(B,),
        in_specs=[
            pl.BlockSpec((1, S, W), lambda b: (b, 0, 0)),
            pl.BlockSpec((1, S, W), lambda b: (b, 0, 1)),
            pl.BlockSpec((1, S, W), lambda b: (b, 0, 2)),
            pl.BlockSpec((1, S, W), lambda b: (b, 0, 0)),
            pl.BlockSpec((2, W), lambda b: (0, 0)),
            pl.BlockSpec((1, W), lambda b: (0, 0)),
        ],
        out_specs=pl.BlockSpec((1, S, W), lambda b: (b, 0, 0)),
        out_shape=jax.ShapeDtypeStruct((B, S, W), BF16),
        scratch_shapes=[pltpu.VMEM((SUPER, W), F32)] * 3 + [pltpu.VMEM((HG_HEADS, LANES, LANES), F32)],
        compiler_params=_cparams(("arbitrary",)),
        name="hgrn2",
    )(u3, u3, u3, hgf3, lb_logits, gain_w)


def _outproj_kernel(oa_ref, ob_ref, x_ref, wo_ref, g_ref, wr_ref, br_ref, h_ref, hn_ref, gates_ref):
    half = oa_ref.shape[1]
    h = x_ref[...] + _dot(oa_ref[...], wo_ref[0:half, :]) + _dot(ob_ref[...], wo_ref[half:2 * half, :])
    h_ref[...] = h
    ms = jnp.mean(h * h, axis=-1, keepdims=True)
    hn = (h * lax.rsqrt(ms + NORM_EPS)) * g_ref[...]
    hn_ref[...] = hn.astype(BF16)
    hn_hi = hn.astype(BF16)
    hn_lo = (hn - hn_hi.astype(F32)).astype(BF16)
    wr = wr_ref[...]
    wr_hi = wr.astype(BF16)
    wr_lo = (wr - wr_hi.astype(F32)).astype(BF16)
    logits = _dot(hn_hi, wr_hi) + _dot(hn_lo, wr_hi) + _dot(hn_hi, wr_lo) + br_ref[...]
    lt = logits.T
    g = [lt[i:i + 1, :] for i in range(N_GROUPS)]
    gm = jnp.maximum(jnp.maximum(g[0], g[1]), jnp.maximum(g[2], g[3]))
    gsel = jnp.where(g[0] == gm, 0, jnp.where(g[1] == gm, 1, jnp.where(g[2] == gm, 2, 3)))
    pg = 1.0 / (jnp.exp(g[0] - gm) + jnp.exp(g[1] - gm) + jnp.exp(g[2] - gm) + jnp.exp(g[3] - gm))
    e = []
    for i in range(PER_GROUP):
        rows = [lt[N_GROUPS + PER_GROUP * gg + i:N_GROUPS + PER_GROUP * gg + i + 1, :] for gg in range(N_GROUPS)]
        e.append(jnp.where(gsel == 0, rows[0], jnp.where(gsel == 1, rows[1], jnp.where(gsel == 2, rows[2], rows[3]))))
    e1 = jnp.maximum(jnp.maximum(e[0], e[1]), jnp.maximum(e[2], e[3]))
    i1 = jnp.where(e[0] == e1, 0, jnp.where(e[1] == e1, 1, jnp.where(e[2] == e1, 2, 3)))
    ex = [jnp.where(i1 == i, -jnp.inf, e[i]) for i in range(PER_GROUP)]
    e2 = jnp.maximum(jnp.maximum(ex[0], ex[1]), jnp.maximum(ex[2], ex[3]))
    i2 = jnp.where(ex[0] == e2, 0, jnp.where(ex[1] == e2, 1, jnp.where(ex[2] == e2, 2, 3)))
    r = jnp.exp(e2 - e1)
    w1 = pg / (1.0 + r)
    w2 = w1 * r
    x1 = gsel * PER_GROUP + i1
    x2 = gsel * PER_GROUP + i2
    ridx = lax.broadcasted_iota(jnp.int32, (LANES, 1), 0)
    gt = jnp.where(ridx == x1, w1, 0.0) + jnp.where(ridx == x2, w2, 0.0)
    gates_ref[...] = gt.T


def _outproj(oa, ob, x2, w_out, ffn_norm, wr, br, tm):
    T, D = x2.shape
    half = oa.shape[1]
    return pl.pallas_call(
        _outproj_kernel,
        grid=(T // tm,),
        in_specs=[
            pl.BlockSpec((tm, half), lambda i: (i, 0)),
            pl.BlockSpec((tm, half), lambda i: (i, 0)),
            pl.BlockSpec((tm, D), lambda i: (i, 0)),
            pl.BlockSpec(w_out.shape, lambda i: (0, 0)),
            pl.BlockSpec((1, D), lambda i: (0, 0)),
            pl.BlockSpec(wr.shape, lambda i: (0, 0)),
            pl.BlockSpec((1, LANES), lambda i: (0, 0)),
        ],
        out_specs=[
            pl.BlockSpec((tm, D), lambda i: (i, 0)),
            pl.BlockSpec((tm, D), lambda i: (i, 0)),
            pl.BlockSpec((tm, LANES), lambda i: (i, 0)),
        ],
        out_shape=[
            jax.ShapeDtypeStruct((T, D), F32),
            jax.ShapeDtypeStruct((T, D), BF16),
            jax.ShapeDtypeStruct((T, LANES), F32),
        ],
        compiler_params=_cparams(("arbitrary",)),
        name="outproj",
    )(oa, ob, x2, w_out, ffn_norm, wr, br)


def _moe_kernel(hn_ref, h_ref, gates_ref, wg_ref, wu_ref, wd_ref, fin_ref, o_ref, acc_ref):
    e = pl.program_id(1)

    @pl.when(e == 0)
    def _():
        acc_ref[...] = h_ref[...]

    lane = lax.broadcasted_iota(jnp.int32, (1, LANES), 1)
    gcol = jnp.sum(jnp.where(lane == e, gates_ref[...], 0.0), axis=-1, keepdims=True)
    t = hn_ref[...]
    a = _dot(t, wg_ref[0].astype(BF16))
    u = _dot(t, wu_ref[0].astype(BF16))
    hid = (a * jax.nn.sigmoid(a)) * u * gcol
    acc_ref[...] += _dot(hid.astype(BF16), wd_ref[0].astype(BF16))

    @pl.when(e == pl.num_programs(1) - 1)
    def _():
        y = acc_ref[...]
        ms = jnp.mean(y * y, axis=-1, keepdims=True)
        o_ref[...] = (y * lax.rsqrt(ms + NORM_EPS)) * fin_ref[...]


def _moe(hn, h, gates, w_gate, w_up, w_down, final_norm, tm):
    T, D = h.shape
    E, _, H = w_gate.shape
    return pl.pallas_call(
        _moe_kernel,
        grid=(T // tm, E),
        in_specs=[
            pl.BlockSpec((tm, D), lambda i, e: (i, 0)),
            pl.BlockSpec((tm, D), lambda i, e: (i, 0)),
            pl.BlockSpec((tm, LANES), lambda i, e: (i, 0)),
            pl.BlockSpec((1, D, H), lambda i, e: (e, 0, 0)),
            pl.BlockSpec((1, D, H), lambda i, e: (e, 0, 0)),
            pl.BlockSpec((1, H, D), lambda i, e: (e, 0, 0)),
            pl.BlockSpec((1, D), lambda i, e: (0, 0)),
        ],
        out_specs=pl.BlockSpec((tm, D), lambda i, e: (i, 0)),
        out_shape=jax.ShapeDtypeStruct((T, D), F32),
        scratch_shapes=[pltpu.VMEM((tm, D), F32)],
        compiler_params=_cparams(("arbitrary", "arbitrary")),
        name="moe",
    )(hn, h, gates, w_gate, w_up, w_down, final_norm)


def kernel(x, attn_norm, w_in, hg_lb_logits, hg_norm, fox_f_bias, fox_norm, w_out, ffn_norm,
           w_group, b_group, w_expert, b_expert, w_gate, w_up, w_down, final_norm):
    B, S, D = x.shape
    T = B * S
    assert w_in.shape[0] == 1, "single-layer block"
    hw = HG_HEADS * HG_DK
    fw = FOX_HEADS * FOX_DH
    wi = w_in[0]
    o = [0, hw, 2 * hw, 3 * hw, 4 * hw, 4 * hw + fw, 4 * hw + 2 * fw, 4 * hw + 3 * fw]
    w_all = jnp.concatenate(
        [wi[:, o[0]:o[1]], wi[:, o[2]:o[3]], wi[:, o[3]:o[4]], wi[:, o[4]:o[7]], wi[:, o[1]:o[2]],
         jnp.pad(wi[:, o[7]:], ((0, 0), (0, LANES - FOX_HEADS)))], axis=1).astype(BF16)

    x2 = x.reshape(T, D)
    tm_in = min(512, S)
    u, hgf, foxf = _inproj(x2, attn_norm.reshape(1, D), w_all, B, S, tm_in)

    cum = _foxcum(foxf.reshape(B * FOX_HEADS, S), jnp.tile(fox_f_bias[0], B).reshape(B * FOX_HEADS, 1),
                  min(256, S))
    tq = min(256, S)
    u3 = u.reshape(B, S, -1)
    o_b = _fox(u3, cum.reshape(B, FOX_HEADS // 2, 2, S // tq, tq),
               jnp.tile(fox_norm[0], 2).reshape(1, LANES), tq)
    o_a = _hgrn2w(u3, hgf.reshape(B, S, hw), hg_lb_logits, jnp.tile(hg_norm[0], HG_HEADS).reshape(1, hw))

    wr = jnp.pad(jnp.concatenate([w_group[0], w_expert[0]], axis=1),
                 ((0, 0), (0, LANES - N_GROUPS - N_EXPERTS)))
    br = jnp.pad(jnp.concatenate([b_group[0], b_expert[0]]), (0, LANES - N_GROUPS - N_EXPERTS)).reshape(1, LANES)
    h, hn, gates = _outproj(o_a.reshape(T, hw), o_b.reshape(T, fw), x2, w_out[0].astype(BF16),
                            ffn_norm[0].reshape(1, D), wr, br, min(512, T))
    out = _moe(hn, h, gates, w_gate[0], w_up[0], w_down[0], final_norm.reshape(1, D), min(1024, T))
    return out.reshape(B, S, D)
```

```python
import functools

import jax
import jax.numpy as jnp
from jax import lax
from jax.experimental import pallas as pl
from jax.experimental.pallas import tpu as pltpu

F32 = jnp.float32
BF16 = jnp.bfloat16

NORM_EPS = 1e-6
NEG_INF = -1e30

LANES = 128
HG_HEADS = 4
HG_DK = 128
FOX_HEADS = 8
FOX_DH = 64
N_GROUPS = 4
PER_GROUP = 4
N_EXPERTS = 16

FOX_STRIP = 32
SUB = 16
SUPER = 128
MOE_TILE = 512
DISP_TM = 512
COMB_TM = 256
VMEM_LIMIT = 56 * 1024 * 1024


def _cparams(sem):
    return pltpu.CompilerParams(dimension_semantics=sem, vmem_limit_bytes=VMEM_LIMIT)


def _split3(x):
    hi = x.astype(BF16)
    r1 = x - hi.astype(F32)
    mid = r1.astype(BF16)
    lo = (r1 - mid.astype(F32)).astype(BF16)
    return hi, mid, lo


def _dot(a, b):
    return jnp.dot(a, b, preferred_element_type=F32)


def _dot_nt(a, b):
    return lax.dot_general(a, b, (((1,), (1,)), ((), ())), preferred_element_type=F32)


def _dot3(m_bf16, x_f32):
    hi, mid, lo = _split3(x_f32)
    return _dot(m_bf16, hi) + _dot(m_bf16, mid) + _dot(m_bf16, lo)


def _inproj_kernel(x_ref, g_ref, w_ref, u_ref, hgf_ref, foxf_ref, xn_ref, *, n_main, col_chunk):
    x = x_ref[...]
    ms = jnp.mean(x * x, axis=-1, keepdims=True)
    xn_ref[...] = ((x * lax.rsqrt(ms + NORM_EPS)) * g_ref[...]).astype(BF16)
    for j in range(n_main // col_chunk):
        r = _dot(xn_ref[...], w_ref[:, j * col_chunk:(j + 1) * col_chunk])
        u_ref[:, j * col_chunk:(j + 1) * col_chunk] = r.astype(BF16)
    r = _dot(xn_ref[...], w_ref[:, n_main:n_main + 4 * LANES])
    hgf_ref[...] = r
    r = _dot(xn_ref[...], w_ref[:, n_main + 4 * LANES:n_main + 5 * LANES])
    foxf_ref[0] = r.T[0:FOX_HEADS, :]


def _inproj(x2, attn_norm, w_all, B, S, tm):
    T, D = x2.shape
    n_main = 6 * 512
    per_seq = S // tm
    return pl.pallas_call(
        functools.partial(_inproj_kernel, n_main=n_main, col_chunk=512),
        grid=(T // tm,),
        in_specs=[
            pl.BlockSpec((tm, D), lambda i: (i, 0)),
            pl.BlockSpec((1, D), lambda i: (0, 0)),
            pl.BlockSpec(w_all.shape, lambda i: (0, 0)),
        ],
        out_specs=[
            pl.BlockSpec((tm, n_main), lambda i: (i, 0)),
            pl.BlockSpec((tm, 4 * LANES), lambda i: (i, 0)),
            pl.BlockSpec((1, FOX_HEADS, tm), lambda i: (i // per_seq, 0, i % per_seq)),
        ],
        out_shape=[
            jax.ShapeDtypeStruct((T, n_main), BF16),
            jax.ShapeDtypeStruct((T, 4 * LANES), F32),
            jax.ShapeDtypeStruct((B, FOX_HEADS, S), F32),
        ],
        scratch_shapes=[pltpu.VMEM((tm, D), BF16)],
        compiler_params=_cparams(("arbitrary",)),
        name="inproj",
    )(x2, attn_norm, w_all)


def _foxcum_kernel(f_ref, bias_ref, cum_ref, off_ref):
    blk = f_ref.shape[1]

    @pl.when(pl.program_id(0) == 0)
    def _():
        off_ref[...] = jnp.zeros_like(off_ref)

    r_i = lax.broadcasted_iota(jnp.int32, (blk, blk), 0)
    c_i = lax.broadcasted_iota(jnp.int32, (blk, blk), 1)
    upper = (r_i <= c_i).astype(BF16)
    z = f_ref[...] + bias_ref[...]
    logf = jnp.minimum(z, 0.0) - jnp.log(1.0 + jnp.exp(-jnp.abs(z)))
    hi, mid, lo = _split3(logf)
    cum = _dot(hi, upper) + _dot(mid, upper) + _dot(lo, upper) + off_ref[...]
    cum_ref[...] = cum
    off_ref[...] = cum[:, blk - 1:blk]


def _foxcum(foxf2, bias_col, blk):
    R, S = foxf2.shape
    return pl.pallas_call(
        _foxcum_kernel,
        grid=(S // blk,),
        in_specs=[pl.BlockSpec((R, blk), lambda j: (0, j)), pl.BlockSpec((R, 1), lambda j: (0, 0))],
        out_specs=pl.BlockSpec((R, blk), lambda j: (0, j)),
        out_shape=jax.ShapeDtypeStruct((R, S), F32),
        scratch_shapes=[pltpu.VMEM((R, 1), F32)],
        compiler_params=_cparams(("arbitrary",)),
        name="foxcum",
    )(foxf2, bias_col)


def _fox_kernel(q_ref, k_ref, v_ref, cum_ref, gain_ref, o_ref, qh_scr, s_scr, p_scr, m_scr, al_scr, acc_scr, *, tq):
    qi = pl.program_id(1)
    n_pair = FOX_HEADS // 2
    lane = lax.broadcasted_iota(jnp.int32, (1, LANES), 1)
    in_h = [lane < FOX_DH, lane >= FOX_DH]
    for p in range(n_pair):
        q = q_ref[0, :, p * LANES:(p + 1) * LANES] * jnp.asarray(FOX_DH ** -0.5, BF16)
        for h in range(2):
            qh_scr[2 * p + h] = jnp.where(in_h[h], q, jnp.zeros_like(q))
    rs = FOX_STRIP
    row = lax.broadcasted_iota(jnp.int32, (rs, tq), 0)
    col = lax.broadcasted_iota(jnp.int32, (rs, tq), 1)

    m_scr[...] = jnp.full(m_scr.shape, NEG_INF, F32)
    acc_scr[...] = jnp.zeros_like(acc_scr)

    def step(kb, masked):
        r0 = pl.multiple_of(kb * tq, tq)
        for hd in range(FOX_HEADS):
            p = hd // 2
            s_scr[hd] = _dot_nt(qh_scr[hd], k_ref[0, pl.ds(r0, tq), p * LANES:(p + 1) * LANES])
        for hd in range(FOX_HEADS):
            bias = -cum_ref[0, hd // 2, hd % 2, pl.ds(kb, 1), :]
            for st in range(tq // rs):
                rows = slice(st * rs, (st + 1) * rs)
                s = s_scr[hd, rows, :] + bias
                if masked:
                    s = jnp.where(row + st * rs >= col, s, NEG_INF)
                m = m_scr[hd, rows, :]
                m_new = jnp.maximum(m, jnp.max(s, axis=-1, keepdims=True))
                p_scr[hd, rows, :] = jnp.exp(s - jnp.concatenate([m_new] * (tq // LANES), axis=1)).astype(BF16)
                al_scr[hd, rows, :] = jnp.exp(m - m_new)
                m_scr[hd, rows, :] = m_new
        for hd in range(FOX_HEADS):
            p = hd // 2
            v_blk = v_ref[0, pl.ds(r0, tq), p * LANES:(p + 1) * LANES]
            vh = jnp.where(in_h[hd % 2], v_blk, jnp.ones_like(v_blk))
            acc_scr[hd] = acc_scr[hd] * al_scr[hd] + _dot(p_scr[hd], vh)

    def body(kb, c):
        step(kb, False)
        return c

    lax.fori_loop(0, qi, body, 0)
    step(qi, True)
    for p in range(n_pair):
        a0, a1 = acc_scr[2 * p], acc_scr[2 * p + 1]
        o0 = a0 / pltpu.roll(a0, FOX_DH, 1)
        o1 = a1 / pltpu.roll(a1, FOX_DH, 1)
        o = jnp.where(in_h[0], o0, o1)
        o2 = o * o
        ms0 = jnp.sum(jnp.where(in_h[0], o2, 0.0), axis=-1, keepdims=True) * (1.0 / FOX_DH)
        ms1 = jnp.sum(jnp.where(in_h[1], o2, 0.0), axis=-1, keepdims=True) * (1.0 / FOX_DH)
        ms = jnp.where(in_h[0], ms0, ms1)
        o_ref[0, :, p * LANES:(p + 1) * LANES] = (o * lax.rsqrt(ms + NORM_EPS) * gain_ref[...]).astype(BF16)


def _fox(u3, cum5, gain2, tq):
    B, S, _ = u3.shape
    nq = S // tq
    fw = FOX_HEADS * FOX_DH
    qc, kc, vc = 3, 4, 5
    return pl.pallas_call(
        functools.partial(_fox_kernel, tq=tq),
        grid=(B, nq),
        in_specs=[
            pl.BlockSpec((1, tq, fw), lambda b, i: (b, i, qc)),
            pl.BlockSpec((1, S, fw), lambda b, i: (b, 0, kc)),
            pl.BlockSpec((1, S, fw), lambda b, i: (b, 0, vc)),
            pl.BlockSpec((1, FOX_HEADS // 2, 2, nq, tq), lambda b, i: (b, 0, 0, 0, 0)),
            pl.BlockSpec((1, LANES), lambda b, i: (0, 0)),
        ],
        out_specs=pl.BlockSpec((1, tq, fw), lambda b, i: (b, i, 0)),
        out_shape=jax.ShapeDtypeStruct((B, S, fw), BF16),
        scratch_shapes=[
            pltpu.VMEM((FOX_HEADS, tq, LANES), BF16),
            pltpu.VMEM((FOX_HEADS, tq, tq), F32),
            pltpu.VMEM((FOX_HEADS, tq, tq), BF16),
            pltpu.VMEM((FOX_HEADS, tq, LANES), F32),
            pltpu.VMEM((FOX_HEADS, tq, LANES), F32),
            pltpu.VMEM((FOX_HEADS, tq, LANES), F32),
        ],
        compiler_params=_cparams(("arbitrary", "arbitrary")),
        name="fox",
    )(u3, u3, u3, cum5, gain2)


def _hgrn2_kernel(q_ref, i_ref, g_ref, f_ref, lbl_ref, gain_ref, o_ref, b_scr, q_scr, k_scr, st_scr):
    S = q_ref.shape[1]
    n_sub = SUPER // SUB
    a = lbl_ref[...]
    am = jnp.max(a, axis=0, keepdims=True)
    ea = jnp.exp(a - am)
    lb = ea[0:1, :] / (ea[0:1, :] + ea[1:2, :])

    r_i = lax.broadcasted_iota(jnp.int32, (SUPER, SUPER), 0)
    c_i = lax.broadcasted_iota(jnp.int32, (SUPER, SUPER), 1)
    same = (r_i // SUB) == (c_i // SUB)
    tri = (same & (c_i <= r_i)).astype(BF16)
    blk1 = same.astype(BF16)
    ones = jnp.ones((LANES, LANES), BF16)
    lane = lax.broadcasted_iota(jnp.int32, (1, LANES), 1)
    trow = lax.broadcasted_iota(jnp.int32, (8, LANES), 0)

    st_scr[...] = jnp.zeros_like(st_scr)

    def superblock(sb, _):
        r0 = pl.multiple_of(sb * SUPER, SUPER)
        qr = q_ref[0, pl.ds(r0, SUPER), :].astype(F32)
        q = qr * jax.nn.sigmoid(qr)
        f = lb + (1.0 - lb) * jax.nn.sigmoid(f_ref[0, pl.ds(r0, SUPER), :])
        logf = jnp.log(f)
        k = 1.0 - f
        v = i_ref[0, pl.ds(r0, SUPER), :]
        b = _dot3(tri, logf)
        btot = _dot3(blk1, logf)
        qt = (q * jnp.exp(b)).astype(BF16)
        kh = (k * jnp.exp(btot - b)).astype(BF16)
        gam = jnp.exp(btot)
        b_scr[...] = b
        q_scr[...] = q
        k_scr[...] = k
        vt = v.astype(F32).T

        st = st_scr[...]
        o_inter = []
        for c in range(n_sub):
            o_inter.append(_dot_nt(qt[c * SUB:(c + 1) * SUB, :], st.astype(BF16)))
            vtm = jnp.where((lane // SUB) == c, vt, 0.0).astype(BF16)
            st = gam[c * SUB:c * SUB + 1, :] * st + _dot(vtm, kh)
        st_scr[...] = st

        a_rows = []
        for c in range(n_sub):
            base = c * SUB
            tiles = []
            lo_half = []
            for j in range(SUB):
                bs = b_scr[base + j:base + j + 1, :]
                ks = k_scr[base + j:base + j + 1, :]
                halves = []
                for hf in range(2):
                    if hf == 0 and j >= 8:
                        continue
                    t0 = base + 8 * hf
                    p = (q_scr[t0:t0 + 8, :] * jnp.exp(b_scr[t0:t0 + 8, :] - bs)) * ks
                    if j // 8 == hf:
                        p = jnp.where(trow >= (j % 8), p, 0.0)
                    halves.append(p)
                if j < 8:
                    tiles.append(jnp.concatenate(halves, axis=0))
                else:
                    lo_half.append(halves[0])
            for m in range(4):
                tiles.append(jnp.concatenate([lo_half[2 * m], lo_half[2 * m + 1]], axis=0))
            pc = jnp.concatenate(tiles, axis=0).astype(BF16)
            rc = _dot(pc, ones)
            a_top = jnp.zeros((8, LANES), F32)
            a_bot = jnp.zeros((8, LANES), F32)
            for j in range(8):
                sel = lane == (base + j)
                a_top = jnp.where(sel, rc[16 * j:16 * j + 8, :], a_top)
                a_bot = jnp.where(sel, rc[16 * j + 8:16 * j + 16, :], a_bot)
            for j in range(8, SUB):
                sel = lane == (base + j)
                a_bot = jnp.where(sel, rc[128 + 8 * (j - 8):128 + 8 * (j - 8) + 8, :], a_bot)
            a_rows += [a_top, a_bot]
        a_blk = jnp.concatenate(a_rows, axis=0).astype(BF16)
        o = jnp.concatenate(o_inter, axis=0) + _dot(a_blk, v)
        ms = jnp.mean(o * o, axis=-1, keepdims=True)
        o = o * lax.rsqrt(ms + NORM_EPS) * gain_ref[...]
        gr = g_ref[0, pl.ds(r0, SUPER), :].astype(F32)
        o_ref[0, pl.ds(r0, SUPER), :] = (o * (gr * jax.nn.sigmoid(gr))).astype(BF16)
        return 0

    lax.fori_loop(0, S // SUPER, superblock, 0)


def _hgrn2(u3, hgf3, lb_logits, gain):
    B, S, _ = u3.shape
    return pl.pallas_call(
        _hgrn2_kernel,
        grid=(B, HG_HEADS),
        in_specs=[
            pl.BlockSpec((1, S, LANES), lambda b, h: (b, 0, h)),
            pl.BlockSpec((1, S, LANES), lambda b, h: (b, 0, 4 + h)),
            pl.BlockSpec((1, S, LANES), lambda b, h: (b, 0, 8 + h)),
            pl.BlockSpec((1, S, LANES), lambda b, h: (b, 0, h)),
            pl.BlockSpec((2, LANES), lambda b, h: (0, h)),
            pl.BlockSpec((1, LANES), lambda b, h: (0, 0)),
        ],
        out_specs=pl.BlockSpec((1, S, LANES), lambda b, h: (b, 0, h)),
        out_shape=jax.ShapeDtypeStruct((B, S, HG_HEADS * HG_DK), BF16),
        scratch_shapes=[pltpu.VMEM((SUPER, LANES), F32)] * 4,
        compiler_params=_cparams(("arbitrary", "arbitrary")),
        name="hgrn2",
    )(u3, u3, u3, hgf3, lb_logits, gain)


def _hgrn2w_kernel(q_ref, i_ref, g_ref, f_ref, lbl_ref, gain_ref, o_ref, b_scr, q_scr, c_scr, st_scr):
    S, W = q_ref.shape[1], q_ref.shape[2]
    nh = W // LANES
    n_sub = SUPER // SUB
    hsl = [slice(h * LANES, (h + 1) * LANES) for h in range(nh)]
    a = lbl_ref[...]
    am = jnp.max(a, axis=0, keepdims=True)
    ea = jnp.exp(a - am)
    lb = ea[0:1, :] / (ea[0:1, :] + ea[1:2, :])

    r_i = lax.broadcasted_iota(jnp.int32, (SUPER, SUPER), 0)
    c_i = lax.broadcasted_iota(jnp.int32, (SUPER, SUPER), 1)
    same = (r_i // SUB) == (c_i // SUB)
    tri = (same & (c_i <= r_i)).astype(BF16)
    blk1 = same.astype(BF16)
    ones = jnp.ones((LANES, LANES), BF16)
    lane = lax.broadcasted_iota(jnp.int32, (1, LANES), 1)
    trow = lax.broadcasted_iota(jnp.int32, (8, W), 0)

    st_scr[...] = jnp.zeros_like(st_scr)

    def superblock(sb, _):
        r0 = pl.multiple_of(sb * SUPER, SUPER)
        qr = q_ref[0, pl.ds(r0, SUPER), :].astype(F32)
        q = qr * jax.nn.sigmoid(qr)
        f = lb + (1.0 - lb) * jax.nn.sigmoid(f_ref[0, pl.ds(r0, SUPER), :])
        lf2 = jnp.log2(f)
        b2 = _dot3(tri, lf2)
        bt2 = _dot3(blk1, lf2)
        c2 = b2 - jnp.log2(1.0 - f)
        qt = (q * jnp.exp2(b2)).astype(BF16)
        kh = jnp.exp2(bt2 - c2).astype(BF16)
        gam = jnp.exp2(bt2)
        b_scr[...] = b2
        q_scr[...] = q
        c_scr[...] = c2
        v = i_ref[0, pl.ds(r0, SUPER), :]
        vt = [v[:, hsl[h]].astype(F32).T for h in range(nh)]

        st = [st_scr[h] for h in range(nh)]
        o_inter = [[] for _ in range(nh)]
        for c in range(n_sub):
            rows = slice(c * SUB, (c + 1) * SUB)
            for h in range(nh):
                o_inter[h].append(_dot_nt(qt[rows, hsl[h]], st[h].astype(BF16)))
                dst = lax.dot_general(v[rows, hsl[h]], kh[rows, hsl[h]], (((0,), (0,)), ((), ())),
                                      preferred_element_type=F32)
                st[h] = gam[c * SUB:c * SUB + 1, hsl[h]] * st[h] + dst
        for h in range(nh):
            st_scr[h] = st[h]

        a_rows = [[] for _ in range(nh)]
        for c in range(n_sub):
            base = c * SUB
            tiles = []
            lo_half = []
            for j in range(SUB):
                cs = c_scr[base + j:base + j + 1, :]
                halves = []
                for hf in range(2):
                    if hf == 0 and j >= 8:
                        continue
                    t0 = base + 8 * hf
                    p = q_scr[t0:t0 + 8, :] * jnp.exp2(b_scr[t0:t0 + 8, :] - cs)
                    if j // 8 == hf:
                        p = jnp.where(trow >= (j % 8), p, 0.0)
                    halves.append(p)
                if j < 8:
                    tiles.append(jnp.concatenate(halves, axis=0))
                else:
                    lo_half.append(halves[0])
            for m in range(4):
                tiles.append(jnp.concatenate([lo_half[2 * m], lo_half[2 * m + 1]], axis=0))
            pc = jnp.concatenate(tiles, axis=0).astype(BF16)
            for h in range(nh):
                rc = _dot(pc[:, hsl[h]], ones)
                a_top = jnp.zeros((8, LANES), F32)
                a_bot = jnp.zeros((8, LANES), F32)
                for j in range(8):
                    sel = lane == (base + j)
                    a_top = jnp.where(sel, rc[16 * j:16 * j + 8, :], a_top)
                    a_bot = jnp.where(sel, rc[16 * j + 8:16 * j + 16, :], a_bot)
                for j in range(8, SUB):
                    sel = lane == (base + j)
                    a_bot = jnp.where(sel, rc[128 + 8 * (j - 8):128 + 8 * (j - 8) + 8, :], a_bot)
                a_rows[h] += [a_top, a_bot]

        outs = []
        for h in range(nh):
            a_blk = jnp.concatenate(a_rows[h], axis=0).astype(BF16)
            o = jnp.concatenate(o_inter[h], axis=0) + _dot(a_blk, v[:, hsl[h]])
            ms = jnp.mean(o * o, axis=-1, keepdims=True)
            outs.append(o * lax.rsqrt(ms + NORM_EPS))
        o = jnp.concatenate(outs, axis=1) * gain_ref[...]
        gr = g_ref[0, pl.ds(r0, SUPER), :].astype(F32)
        o_ref[0, pl.ds(r0, SUPER), :] = (o * (gr * jax.nn.sigmoid(gr))).astype(BF16)
        return 0

    lax.fori_loop(0, S // SUPER, superblock, 0)


def _hgrn2w(u3, hgf3, lb_logits, gain_w):
    B, S, _ = u3.shape
    W = HG_HEADS * HG_DK
    return pl.pallas_call(
        _hgrn2w_kernel,
        grid=(B,),
        in_specs=[
            pl.BlockSpec((1, S, W), lambda b: (b, 0, 0)),
            pl.BlockSpec((1, S, W), lambda b: (b, 0, 1)),
            pl.BlockSpec((1, S, W), lambda b: (b, 0, 2)),
            pl.BlockSpec((1, S, W), lambda b: (b, 0, 0)),
            pl.BlockSpec((2, W), lambda b: (0, 0)),
            pl.BlockSpec((1, W), lambda b: (0, 0)),
        ],
        out_specs=pl.BlockSpec((1, S, W), lambda b: (b, 0, 0)),
        out_shape=jax.ShapeDtypeStruct((B, S, W), BF16),
        scratch_shapes=[pltpu.VMEM((SUPER, W), F32)] * 3 + [pltpu.VMEM((HG_HEADS, LANES, LANES), F32)],
        compiler_params=_cparams(("arbitrary",)),
        name="hgrn2",
    )(u3, u3, u3, hgf3, lb_logits, gain_w)


def _rows_to_tiles(ref, words):
    for c in range(ref.shape[-2]):
        ref[:, c, :] = words[:, c * LANES:(c + 1) * LANES]


def _tiles_to_rows(ref):
    return jnp.concatenate([ref[:, c, :] for c in range(ref.shape[-2])], axis=1)


def _outproj_kernel(oa_ref, ob_ref, x_ref, wo_ref, g_ref, wr_ref, br_ref, h_ref, hn_ref, rt_ref, r8_ref):
    half = oa_ref.shape[1]
    h = x_ref[...] + _dot(oa_ref[...], wo_ref[0:half, :]) + _dot(ob_ref[...], wo_ref[half:2 * half, :])
    h_ref[...] = h
    ms = jnp.mean(h * h, axis=-1, keepdims=True)
    hn = (h * lax.rsqrt(ms + NORM_EPS)) * g_ref[...]
    _rows_to_tiles(hn_ref, hn)
    hn_hi = hn.astype(BF16)
    hn_lo = (hn - hn_hi.astype(F32)).astype(BF16)
    wr = wr_ref[...]
    wr_hi = wr.astype(BF16)
    wr_lo = (wr - wr_hi.astype(F32)).astype(BF16)
    logits = _dot(hn_hi, wr_hi) + _dot(hn_lo, wr_hi) + _dot(hn_hi, wr_lo) + br_ref[...]
    lt = logits.T
    g = [lt[i:i + 1, :] for i in range(N_GROUPS)]
    gm = jnp.maximum(jnp.maximum(g[0], g[1]), jnp.maximum(g[2], g[3]))
    gsel = jnp.where(g[0] == gm, 0, jnp.where(g[1] == gm, 1, jnp.where(g[2] == gm, 2, 3)))
    pg = 1.0 / (jnp.exp(g[0] - gm) + jnp.exp(g[1] - gm) + jnp.exp(g[2] - gm) + jnp.exp(g[3] - gm))
    e = []
    for i in range(PER_GROUP):
        rows = [lt[N_GROUPS + PER_GROUP * gg + i:N_GROUPS + PER_GROUP * gg + i + 1, :] for gg in range(N_GROUPS)]
        e.append(jnp.where(gsel == 0, rows[0], jnp.where(gsel == 1, rows[1], jnp.where(gsel == 2, rows[2], rows[3]))))
    e1 = jnp.maximum(jnp.maximum(e[0], e[1]), jnp.maximum(e[2], e[3]))
    i1 = jnp.where(e[0] == e1, 0, jnp.where(e[1] == e1, 1, jnp.where(e[2] == e1, 2, 3)))
    ex = [jnp.where(i1 == i, -jnp.inf, e[i]) for i in range(PER_GROUP)]
    e2 = jnp.maximum(jnp.maximum(ex[0], ex[1]), jnp.maximum(ex[2], ex[3]))
    i2 = jnp.where(ex[0] == e2, 0, jnp.where(ex[1] == e2, 1, jnp.where(ex[2] == e2, 2, 3)))
    r = jnp.exp(e2 - e1)
    w1 = pg / (1.0 + r)
    w2 = w1 * r
    x1 = gsel * PER_GROUP + i1
    x2 = gsel * PER_GROUP + i2
    ridx = lax.broadcasted_iota(jnp.int32, (LANES, 1), 0)
    wt = jnp.where(ridx == 0, w1, 0.0) + jnp.where(ridx == 1, w2, 0.0)
    rt_ref[...] = wt.T
    sidx = lax.broadcasted_iota(jnp.int32, (8, 1), 0)
    r8_ref[...] = (jnp.where(sidx == 0, x1, 0) + jnp.where(sidx == 1, x2, 0)).astype(F32)


def _outproj(oa, ob, x2, w_out, ffn_norm, wr, br, tm):
    T, D = x2.shape
    half = oa.shape[1]
    return pl.pallas_call(
        _outproj_kernel,
        grid=(T // tm,),
        in_specs=[
            pl.BlockSpec((tm, half), lambda i: (i, 0)),
            pl.BlockSpec((tm, half), lambda i: (i, 0)),
            pl.BlockSpec((tm, D), lambda i: (i, 0)),
            pl.BlockSpec(w_out.shape, lambda i: (0, 0)),
            pl.BlockSpec((1, D), lambda i: (0, 0)),
            pl.BlockSpec(wr.shape, lambda i: (0, 0)),
            pl.BlockSpec((1, LANES), lambda i: (0, 0)),
        ],
        out_specs=[
            pl.BlockSpec((tm, D), lambda i: (i, 0)),
            pl.BlockSpec((tm, D // LANES, LANES), lambda i: (i, 0, 0)),
            pl.BlockSpec((tm, LANES), lambda i: (i, 0)),
            pl.BlockSpec((8, tm), lambda i: (0, i)),
        ],
        out_shape=[
            jax.ShapeDtypeStruct((T, D), F32),
            jax.ShapeDtypeStruct((T, D // LANES, LANES), F32),
            jax.ShapeDtypeStruct((T, LANES), F32),
            jax.ShapeDtypeStruct((8, T), F32),
        ],
        compiler_params=_cparams(("arbitrary",)),
        name="outproj",
    )(oa, ob, x2, w_out, ffn_norm, wr, br)


def _rank_kernel(r8_ref, rank_ref, cnt_ref, off_scr):
    blk = r8_ref.shape[1]

    @pl.when(pl.program_id(0) == 0)
    def _():
        off_scr[...] = jnp.zeros_like(off_scr)

    x1, x2 = r8_ref[0:1, :], r8_ref[1:2, :]
    eid = lax.broadcasted_iota(jnp.int32, (N_EXPERTS, 1), 0).astype(F32)
    r_i = lax.broadcasted_iota(jnp.int32, (blk, blk), 0)
    c_i = lax.broadcasted_iota(jnp.int32, (blk, blk), 1)
    upper = (r_i <= c_i).astype(BF16)
    hit = ((x1 == eid) | (x2 == eid)).astype(BF16)
    cum = _dot(hit, upper) + off_scr[...]
    rank1 = jnp.sum(jnp.where(x1 == eid, cum, 0.0), axis=0, keepdims=True) - 1.0
    rank2 = jnp.sum(jnp.where(x2 == eid, cum, 0.0), axis=0, keepdims=True) - 1.0
    sidx = lax.broadcasted_iota(jnp.int32, (8, 1), 0)
    rank_ref[...] = jnp.where(sidx == 0, rank1, 0.0) + jnp.where(sidx == 1, rank2, 0.0)
    off_scr[...] = cum[:, blk - 1:blk]
    cnt_ref[...] = jnp.broadcast_to(cum[:, blk - 1:blk], cnt_ref.shape)


def _rank(r8, blk):
    T = r8.shape[1]
    return pl.pallas_call(
        _rank_kernel,
        grid=(T // blk,),
        in_specs=[pl.BlockSpec((8, blk), lambda j: (0, j))],
        out_specs=[pl.BlockSpec((8, blk), lambda j: (0, j)), pl.BlockSpec((N_EXPERTS, LANES), lambda j: (0, 0))],
        out_shape=[jax.ShapeDtypeStruct((8, T), F32), jax.ShapeDtypeStruct((N_EXPERTS, LANES), F32)],
        scratch_shapes=[pltpu.VMEM((N_EXPERTS, 1), F32)],
        compiler_params=_cparams(("arbitrary",)),
        name="moe_rank",
    )(r8)


def _dest_kernel(r8_ref, rank_ref, cnt_ref, dest_ref, meta_ref, te_ref, *, tile):
    cnt = cnt_ref[...]
    cntp = jnp.floor((cnt + (tile - 1.0)) * (1.0 / tile)) * tile
    r_i = lax.broadcasted_iota(jnp.int32, (N_EXPERTS, N_EXPERTS), 0)
    c_i = lax.broadcasted_iota(jnp.int32, (N_EXPERTS, N_EXPERTS), 1)
    off = _dot3((c_i < r_i).astype(BF16), cntp)
    offe = off + cntp
    x1, x2 = r8_ref[0:1, :], r8_ref[1:2, :]
    eid = lax.broadcasted_iota(jnp.int32, (N_EXPERTS, 1), 0).astype(F32)
    off_c = off[:, 0:1]
    d1 = jnp.sum(jnp.where(x1 == eid, off_c, 0.0), axis=0, keepdims=True) + rank_ref[0:1, :]
    d2 = jnp.sum(jnp.where(x2 == eid, off_c, 0.0), axis=0, keepdims=True) + rank_ref[1:2, :]
    sidx = lax.broadcasted_iota(jnp.int32, (8, 1), 0)
    dest_ref[...] = (jnp.where(sidx == 0, d1, 0.0) + jnp.where(sidx == 1, d2, 0.0)).astype(jnp.int32)
    lane = lax.broadcasted_iota(jnp.int32, (1, LANES), 1)
    meta_ref[...] = jnp.where(lane == 0, off, offe).astype(jnp.int32)
    start = lax.broadcasted_iota(jnp.int32, (1, te_ref.shape[1]), 1).astype(F32) * tile
    te = jnp.sum((start >= offe[:, 0:1]).astype(F32), axis=0, keepdims=True)
    nvalid = offe[N_EXPERTS - 1:N_EXPERTS, 0:1] * (1.0 / tile)
    te_ref[...] = (jnp.where(sidx == 0, jnp.minimum(te, N_EXPERTS - 1.0), 0.0)
                   + jnp.where(sidx == 1, nvalid, 0.0)).astype(jnp.int32)


def _dest(r8, rank, cnt, blk, tile, n_tiles_pad):
    T = r8.shape[1]
    return pl.pallas_call(
        functools.partial(_dest_kernel, tile=float(tile)),
        grid=(T // blk,),
        in_specs=[pl.BlockSpec((8, blk), lambda j: (0, j)), pl.BlockSpec((8, blk), lambda j: (0, j)),
                  pl.BlockSpec((N_EXPERTS, LANES), lambda j: (0, 0))],
        out_specs=[pl.BlockSpec((8, blk), lambda j: (0, j)), pl.BlockSpec((N_EXPERTS, LANES), lambda j: (0, 0)),
                   pl.BlockSpec((8, n_tiles_pad), lambda j: (0, 0))],
        out_shape=[jax.ShapeDtypeStruct((8, T), jnp.int32), jax.ShapeDtypeStruct((N_EXPERTS, LANES), jnp.int32),
                   jax.ShapeDtypeStruct((8, n_tiles_pad), jnp.int32)],
        compiler_params=_cparams(("arbitrary",)),
        name="moe_dest",
    )(r8, rank, cnt)


def _dispatch_kernel(off_ref, offe_ref, dest_ref, hn_ref, xs_hbm, zero_scr, sem, zsem, *, tm, tile):
    i = pl.program_id(0)

    @pl.when(i == 0)
    def _():
        zero_scr[...] = jnp.zeros_like(zero_scr)
        for e in range(N_EXPERTS):
            @pl.when(offe_ref[e] > off_ref[e])
            def _():
                c = pltpu.make_async_copy(zero_scr, xs_hbm.at[pl.ds(offe_ref[e] - tile, tile)], zsem)
                c.start()
                c.wait()

        def fill_unused(j, c):
            cp = pltpu.make_async_copy(zero_scr, xs_hbm.at[pl.ds(j * tile, tile)], zsem)
            cp.start()
            cp.wait()
            return c

        lax.fori_loop(offe_ref[N_EXPERTS - 1] // tile, xs_hbm.shape[0] // tile, fill_unused, 0)

    def issue(r, c):
        for s in range(2):
            pltpu.make_async_copy(hn_ref.at[pl.ds(r, 1)], xs_hbm.at[pl.ds(dest_ref[0, s, r], 1)], sem).start(priority=s)
        return c

    lax.fori_loop(0, tm, issue, 0, unroll=8)

    def drain(r, c):
        for s in range(2):
            pltpu.make_async_copy(hn_ref.at[pl.ds(0, 1)], xs_hbm.at[pl.ds(0, 1)], sem).wait()
        return c

    lax.fori_loop(0, tm, drain, 0, unroll=8)


def _dispatch(off, offe, dest3, hn3, n_rows, tile):
    nt, _, tm = dest3.shape
    row = hn3.shape[1:]
    return pl.pallas_call(
        functools.partial(_dispatch_kernel, tm=tm, tile=tile),
        grid_spec=pltpu.PrefetchScalarGridSpec(
            num_scalar_prefetch=2,
            grid=(nt,),
            in_specs=[pl.BlockSpec((1, 2, tm), lambda i, o, oe: (i, 0, 0), memory_space=pltpu.SMEM),
                      pl.BlockSpec((tm,) + row, lambda i, o, oe: (i, 0, 0))],
            out_specs=pl.BlockSpec(memory_space=pl.ANY),
            scratch_shapes=[pltpu.VMEM((tile,) + row, F32), pltpu.SemaphoreType.DMA, pltpu.SemaphoreType.DMA],
        ),
        out_shape=jax.ShapeDtypeStruct((n_rows,) + row, F32),
        compiler_params=_cparams(("arbitrary",)),
        name="moe_dispatch",
    )(off, offe, dest3, hn3)


def _experts_kernel(te_ref, nv_ref, xs_ref, wg_ref, wu_ref, wd_ref, ys_ref):
    @pl.when(pl.program_id(0) < nv_ref[0])
    def _():
        x = _tiles_to_rows(xs_ref).astype(BF16)
        a = _dot(x, wg_ref[0])
        u = _dot(x, wu_ref[0])
        hid = ((a * jax.nn.sigmoid(a)) * u).astype(BF16)
        _rows_to_tiles(ys_ref, _dot(hid, wd_ref[0]))

    @pl.when(pl.program_id(0) >= nv_ref[0])
    def _():
        ys_ref[...] = jnp.zeros_like(ys_ref)


def _experts(te, nv, xs, wg, wu, wd, tile):
    n_rows = xs.shape[0]
    row = xs.shape[1:]
    E, D, H = wg.shape

    def row_map(j, te, nv):
        return (j, 0, 0)

    return pl.pallas_call(
        _experts_kernel,
        grid_spec=pltpu.PrefetchScalarGridSpec(
            num_scalar_prefetch=2,
            grid=(n_rows // tile,),
            in_specs=[pl.BlockSpec((tile,) + row, row_map),
                      pl.BlockSpec((1, D, H), lambda j, te, nv: (te[j], 0, 0)),
                      pl.BlockSpec((1, D, H), lambda j, te, nv: (te[j], 0, 0)),
                      pl.BlockSpec((1, H, D), lambda j, te, nv: (te[j], 0, 0))],
            out_specs=pl.BlockSpec((tile,) + row, row_map),
        ),
        out_shape=jax.ShapeDtypeStruct(xs.shape, F32),
        compiler_params=_cparams(("arbitrary",)),
        name="moe_experts",
    )(te, nv, xs, wg, wu, wd)


def _combine_kernel(dcur_ref, dnxt_ref, ys_hbm, h_ref, rt_ref, fin_ref, o_ref, buf, sem, *, tm):
    i = pl.program_id(0)
    n = pl.num_programs(0)
    slot = i % 2

    def issue(d_ref, sl):
        def body(r, c):
            for s in range(2):
                pltpu.make_async_copy(ys_hbm.at[pl.ds(d_ref[0, s, r], 1)], buf.at[sl, s, pl.ds(r, 1)],
                                      sem.at[sl]).start(priority=s)
            return c
        lax.fori_loop(0, tm, body, 0, unroll=8)

    @pl.when(i == 0)
    def _():
        issue(dcur_ref, 0)

    @pl.when(i + 1 < n)
    def _():
        issue(dnxt_ref, 1 - slot)

    def drain(r, c):
        for s in range(2):
            pltpu.make_async_copy(ys_hbm.at[pl.ds(0, 1)], buf.at[slot, s, pl.ds(0, 1)], sem.at[slot]).wait()
        return c

    lax.fori_loop(0, tm, drain, 0, unroll=8)
    y = (h_ref[...] + rt_ref[:, 0:1] * _tiles_to_rows(buf.at[slot, 0])
         + rt_ref[:, 1:2] * _tiles_to_rows(buf.at[slot, 1]))
    ms = jnp.mean(y * y, axis=-1, keepdims=True)
    o_ref[...] = (y * lax.rsqrt(ms + NORM_EPS)) * fin_ref[...]


def _combine(dest3, ys, h, rt, final_norm):
    nt, _, tm = dest3.shape
    T, D = h.shape
    row = ys.shape[1:]
    return pl.pallas_call(
        functools.partial(_combine_kernel, tm=tm),
        grid=(nt,),
        in_specs=[pl.BlockSpec((1, 2, tm), lambda i: (i, 0, 0), memory_space=pltpu.SMEM),
                  pl.BlockSpec((1, 2, tm), lambda i: (jnp.minimum(i + 1, nt - 1), 0, 0), memory_space=pltpu.SMEM),
                  pl.BlockSpec(memory_space=pl.ANY),
                  pl.BlockSpec((tm, D), lambda i: (i, 0)),
                  pl.BlockSpec((tm, LANES), lambda i: (i, 0)),
                  pl.BlockSpec((1, D), lambda i: (0, 0))],
        out_specs=pl.BlockSpec((tm, D), lambda i: (i, 0)),
        out_shape=jax.ShapeDtypeStruct((T, D), F32),
        scratch_shapes=[pltpu.VMEM((2, 2, tm) + row, F32), pltpu.SemaphoreType.DMA((2,))],
        compiler_params=_cparams(("arbitrary",)),
        name="moe_combine",
    )(dest3, dest3, ys, h, rt, final_norm)


def _moe_kernel(hn_ref, h_ref, gates_ref, wg_ref, wu_ref, wd_ref, fin_ref, o_ref, acc_ref):
    e = pl.program_id(1)

    @pl.when(e == 0)
    def _():
        acc_ref[...] = h_ref[...]

    lane = lax.broadcasted_iota(jnp.int32, (1, LANES), 1)
    gcol = jnp.sum(jnp.where(lane == e, gates_ref[...], 0.0), axis=-1, keepdims=True)
    t = hn_ref[...]
    a = _dot(t, wg_ref[0].astype(BF16))
    u = _dot(t, wu_ref[0].astype(BF16))
    hid = (a * jax.nn.sigmoid(a)) * u * gcol
    acc_ref[...] += _dot(hid.astype(BF16), wd_ref[0].astype(BF16))

    @pl.when(e == pl.num_programs(1) - 1)
    def _():
        y = acc_ref[...]
        ms = jnp.mean(y * y, axis=-1, keepdims=True)
        o_ref[...] = (y * lax.rsqrt(ms + NORM_EPS)) * fin_ref[...]


def _moe(hn, h, gates, w_gate, w_up, w_down, final_norm, tm):
    T, D = h.shape
    E, _, H = w_gate.shape
    return pl.pallas_call(
        _moe_kernel,
        grid=(T // tm, E),
        in_specs=[
            pl.BlockSpec((tm, D), lambda i, e: (i, 0)),
            pl.BlockSpec((tm, D), lambda i, e: (i, 0)),
            pl.BlockSpec((tm, LANES), lambda i, e: (i, 0)),
            pl.BlockSpec((1, D, H), lambda i, e: (e, 0, 0)),
            pl.BlockSpec((1, D, H), lambda i, e: (e, 0, 0)),
            pl.BlockSpec((1, H, D), lambda i, e: (e, 0, 0)),
            pl.BlockSpec((1, D), lambda i, e: (0, 0)),
        ],
        out_specs=pl.BlockSpec((tm, D), lambda i, e: (i, 0)),
        out_shape=jax.ShapeDtypeStruct((T, D), F32),
        scratch_shapes=[pltpu.VMEM((tm, D), F32)],
        compiler_params=_cparams(("arbitrary", "arbitrary")),
        name="moe",
    )(hn, h, gates, w_gate, w_up, w_down, final_norm)


def kernel(x, attn_norm, w_in, hg_lb_logits, hg_norm, fox_f_bias, fox_norm, w_out, ffn_norm,
           w_group, b_group, w_expert, b_expert, w_gate, w_up, w_down, final_norm):
    B, S, D = x.shape
    T = B * S
    assert w_in.shape[0] == 1, "single-layer block"
    hw = HG_HEADS * HG_DK
    fw = FOX_HEADS * FOX_DH
    wi = w_in[0]
    o = [0, hw, 2 * hw, 3 * hw, 4 * hw, 4 * hw + fw, 4 * hw + 2 * fw, 4 * hw + 3 * fw]
    w_all = jnp.concatenate(
        [wi[:, o[0]:o[1]], wi[:, o[2]:o[3]], wi[:, o[3]:o[4]], wi[:, o[4]:o[7]], wi[:, o[1]:o[2]],
         jnp.pad(wi[:, o[7]:], ((0, 0), (0, LANES - FOX_HEADS)))], axis=1).astype(BF16)

    x2 = x.reshape(T, D)
    tm_in = min(512, S)
    u, hgf, foxf = _inproj(x2, attn_norm.reshape(1, D), w_all, B, S, tm_in)

    cum = _foxcum(foxf.reshape(B * FOX_HEADS, S), jnp.tile(fox_f_bias[0], B).reshape(B * FOX_HEADS, 1),
                  min(256, S))
    tq = min(256, S)
    u3 = u.reshape(B, S, -1)
    o_b = _fox(u3, cum.reshape(B, FOX_HEADS // 2, 2, S // tq, tq),
               jnp.tile(fox_norm[0], 2).reshape(1, LANES), tq)
    o_a = _hgrn2w(u3, hgf.reshape(B, S, hw), hg_lb_logits, jnp.tile(hg_norm[0], HG_HEADS).reshape(1, hw))

    wr = jnp.pad(jnp.concatenate([w_group[0], w_expert[0]], axis=1),
                 ((0, 0), (0, LANES - N_GROUPS - N_EXPERTS)))
    br = jnp.pad(jnp.concatenate([b_group[0], b_expert[0]]), (0, LANES - N_GROUPS - N_EXPERTS)).reshape(1, LANES)
    h, hn3, rt, r8 = _outproj(o_a.reshape(T, hw), o_b.reshape(T, fw), x2, w_out[0].astype(BF16),
                              ffn_norm[0].reshape(1, D), wr, br, min(512, T))

    blk = min(512, T)
    n_rows = 2 * T + N_EXPERTS * MOE_TILE
    n_tiles = n_rows // MOE_TILE
    rank, cnt = _rank(r8, blk)
    dest, meta, te8 = _dest(r8, rank, cnt, blk, MOE_TILE, -(-n_tiles // LANES) * LANES)
    off, offe = meta[:, 0], meta[:, 1]
    te, nv = te8[0, :n_tiles], te8[1, :1]

    def tiles_of(tm):
        return dest[0:2].reshape(2, T // tm, tm).transpose(1, 0, 2)

    xs = _dispatch(off, offe, tiles_of(min(DISP_TM, T)), hn3, n_rows, MOE_TILE)
    ys = _experts(te, nv, xs, w_gate[0].astype(BF16), w_up[0].astype(BF16), w_down[0].astype(BF16), MOE_TILE)
    out = _combine(tiles_of(min(COMB_TM, T)), ys, h, rt, final_norm.reshape(1, D))
    return out.reshape(B, S, D)
```

```python
import functools

import jax
import jax.numpy as jnp
from jax import lax
from jax.experimental import pallas as pl
from jax.experimental.pallas import tpu as pltpu

F32 = jnp.float32
BF16 = jnp.bfloat16

NORM_EPS = 1e-6
NEG_INF = -1e30

LANES = 128
HG_HEADS = 4
HG_DK = 128
FOX_HEADS = 8
FOX_DH = 64
N_GROUPS = 4
PER_GROUP = 4
N_EXPERTS = 16

FOX_STRIP = 32
SUB = 16
SUPER = 128
HG_STEP = 2
MOE_TILE = 512
DISP_TM = 512
COMB_TM = 256
VMEM_LIMIT = 56 * 1024 * 1024


def _cparams(sem):
    return pltpu.CompilerParams(dimension_semantics=sem, vmem_limit_bytes=VMEM_LIMIT)


def _split3(x):
    hi = x.astype(BF16)
    r1 = x - hi.astype(F32)
    mid = r1.astype(BF16)
    lo = (r1 - mid.astype(F32)).astype(BF16)
    return hi, mid, lo


def _dot(a, b):
    return jnp.dot(a, b, preferred_element_type=F32)


def _dot_nt(a, b):
    return lax.dot_general(a, b, (((1,), (1,)), ((), ())), preferred_element_type=F32)


def _dot3(m_bf16, x_f32):
    hi, mid, lo = _split3(x_f32)
    return _dot(m_bf16, hi) + _dot(m_bf16, mid) + _dot(m_bf16, lo)


def _inproj_kernel(x_ref, g_ref, w_ref, u_ref, hgf_ref, foxf_ref, xn_ref, *, n_main, col_chunk):
    x = x_ref[...]
    ms = jnp.mean(x * x, axis=-1, keepdims=True)
    xn_ref[...] = ((x * lax.rsqrt(ms + NORM_EPS)) * g_ref[...]).astype(BF16)
    for j in range(n_main // col_chunk):
        r = _dot(xn_ref[...], w_ref[:, j * col_chunk:(j + 1) * col_chunk])
        u_ref[:, j * col_chunk:(j + 1) * col_chunk] = r.astype(BF16)
    r = _dot(xn_ref[...], w_ref[:, n_main:n_main + 4 * LANES])
    hgf_ref[...] = r
    r = _dot(xn_ref[...], w_ref[:, n_main + 4 * LANES:n_main + 5 * LANES])
    foxf_ref[0] = r.T[0:FOX_HEADS, :]


def _inproj(x2, attn_norm, w_all, B, S, tm):
    T, D = x2.shape
    n_main = 6 * 512
    per_seq = S // tm
    return pl.pallas_call(
        functools.partial(_inproj_kernel, n_main=n_main, col_chunk=512),
        grid=(T // tm,),
        in_specs=[
            pl.BlockSpec((tm, D), lambda i: (i, 0)),
            pl.BlockSpec((1, D), lambda i: (0, 0)),
            pl.BlockSpec(w_all.shape, lambda i: (0, 0)),
        ],
        out_specs=[
            pl.BlockSpec((tm, n_main), lambda i: (i, 0)),
            pl.BlockSpec((tm, 4 * LANES), lambda i: (i, 0)),
            pl.BlockSpec((1, FOX_HEADS, tm), lambda i: (i // per_seq, 0, i % per_seq)),
        ],
        out_shape=[
            jax.ShapeDtypeStruct((T, n_main), BF16),
            jax.ShapeDtypeStruct((T, 4 * LANES), F32),
            jax.ShapeDtypeStruct((B, FOX_HEADS, S), F32),
        ],
        scratch_shapes=[pltpu.VMEM((tm, D), BF16)],
        compiler_params=_cparams(("arbitrary",)),
        name="inproj",
    )(x2, attn_norm, w_all)


def _foxcum_kernel(f_ref, bias_ref, cum_ref, off_ref):
    blk = f_ref.shape[1]

    @pl.when(pl.program_id(0) == 0)
    def _():
        off_ref[...] = jnp.zeros_like(off_ref)

    r_i = lax.broadcasted_iota(jnp.int32, (blk, blk), 0)
    c_i = lax.broadcasted_iota(jnp.int32, (blk, blk), 1)
    upper = (r_i <= c_i).astype(BF16)
    z = f_ref[...] + bias_ref[...]
    logf = jnp.minimum(z, 0.0) - jnp.log(1.0 + jnp.exp(-jnp.abs(z)))
    hi, mid, lo = _split3(logf)
    cum = _dot(hi, upper) + _dot(mid, upper) + _dot(lo, upper) + off_ref[...]
    cum_ref[...] = cum
    off_ref[...] = cum[:, blk - 1:blk]


def _foxcum(foxf2, bias_col, blk):
    R, S = foxf2.shape
    return pl.pallas_call(
        _foxcum_kernel,
        grid=(S // blk,),
        in_specs=[pl.BlockSpec((R, blk), lambda j: (0, j)), pl.BlockSpec((R, 1), lambda j: (0, 0))],
        out_specs=pl.BlockSpec((R, blk), lambda j: (0, j)),
        out_shape=jax.ShapeDtypeStruct((R, S), F32),
        scratch_shapes=[pltpu.VMEM((R, 1), F32)],
        compiler_params=_cparams(("arbitrary",)),
        name="foxcum",
    )(foxf2, bias_col)


def _fox_kernel(q_ref, k_ref, v_ref, cum_ref, gain_ref, o_ref, qh_scr, s_scr, p_scr, m_scr, al_scr, acc_scr, *, tq):
    qi = pl.program_id(1)
    n_pair = FOX_HEADS // 2
    lane = lax.broadcasted_iota(jnp.int32, (1, LANES), 1)
    in_h = [lane < FOX_DH, lane >= FOX_DH]
    for p in range(n_pair):
        q = q_ref[0, :, p * LANES:(p + 1) * LANES] * jnp.asarray(FOX_DH ** -0.5, BF16)
        for h in range(2):
            qh_scr[2 * p + h] = jnp.where(in_h[h], q, jnp.zeros_like(q))
    rs = FOX_STRIP
    row = lax.broadcasted_iota(jnp.int32, (rs, tq), 0)
    col = lax.broadcasted_iota(jnp.int32, (rs, tq), 1)

    m_scr[...] = jnp.full(m_scr.shape, NEG_INF, F32)
    acc_scr[...] = jnp.zeros_like(acc_scr)

    def step(kb, masked):
        r0 = pl.multiple_of(kb * tq, tq)
        for hd in range(FOX_HEADS):
            p = hd // 2
            s_scr[hd] = _dot_nt(qh_scr[hd], k_ref[0, pl.ds(r0, tq), p * LANES:(p + 1) * LANES])
        for hd in range(FOX_HEADS):
            bias = -cum_ref[0, hd // 2, hd % 2, pl.ds(kb, 1), :]
            for st in range(tq // rs):
                rows = slice(st * rs, (st + 1) * rs)
                s = s_scr[hd, rows, :] + bias
                if masked:
                    s = jnp.where(row + st * rs >= col, s, NEG_INF)
                m = m_scr[hd, rows, :]
                m_new = jnp.maximum(m, jnp.max(s, axis=-1, keepdims=True))
                p_scr[hd, rows, :] = jnp.exp(s - jnp.concatenate([m_new] * (tq // LANES), axis=1)).astype(BF16)
                al_scr[hd, rows, :] = jnp.exp(m - m_new)
                m_scr[hd, rows, :] = m_new
        for hd in range(FOX_HEADS):
            p = hd // 2
            v_blk = v_ref[0, pl.ds(r0, tq), p * LANES:(p + 1) * LANES]
            vh = jnp.where(in_h[hd % 2], v_blk, jnp.ones_like(v_blk))
            acc_scr[hd] = acc_scr[hd] * al_scr[hd] + _dot(p_scr[hd], vh)

    def body(kb, c):
        step(kb, False)
        return c

    lax.fori_loop(0, qi, body, 0)
    step(qi, True)
    for p in range(n_pair):
        a0, a1 = acc_scr[2 * p], acc_scr[2 * p + 1]
        o0 = a0 / pltpu.roll(a0, FOX_DH, 1)
        o1 = a1 / pltpu.roll(a1, FOX_DH, 1)
        o = jnp.where(in_h[0], o0, o1)
        o2 = o * o
        ms0 = jnp.sum(jnp.where(in_h[0], o2, 0.0), axis=-1, keepdims=True) * (1.0 / FOX_DH)
        ms1 = jnp.sum(jnp.where(in_h[1], o2, 0.0), axis=-1, keepdims=True) * (1.0 / FOX_DH)
        ms = jnp.where(in_h[0], ms0, ms1)
        o_ref[0, :, p * LANES:(p + 1) * LANES] = (o * lax.rsqrt(ms + NORM_EPS) * gain_ref[...]).astype(BF16)


def _fox(u3, cum5, gain2, tq):
    B, S, _ = u3.shape
    nq = S // tq
    fw = FOX_HEADS * FOX_DH
    qc, kc, vc = 3, 4, 5
    return pl.pallas_call(
        functools.partial(_fox_kernel, tq=tq),
        grid=(B, nq),
        in_specs=[
            pl.BlockSpec((1, tq, fw), lambda b, i: (b, i, qc)),
            pl.BlockSpec((1, S, fw), lambda b, i: (b, 0, kc)),
            pl.BlockSpec((1, S, fw), lambda b, i: (b, 0, vc)),
            pl.BlockSpec((1, FOX_HEADS // 2, 2, nq, tq), lambda b, i: (b, 0, 0, 0, 0)),
            pl.BlockSpec((1, LANES), lambda b, i: (0, 0)),
        ],
        out_specs=pl.BlockSpec((1, tq, fw), lambda b, i: (b, i, 0)),
        out_shape=jax.ShapeDtypeStruct((B, S, fw), BF16),
        scratch_shapes=[
            pltpu.VMEM((FOX_HEADS, tq, LANES), BF16),
            pltpu.VMEM((FOX_HEADS, tq, tq), F32),
            pltpu.VMEM((FOX_HEADS, tq, tq), BF16),
            pltpu.VMEM((FOX_HEADS, tq, LANES), F32),
            pltpu.VMEM((FOX_HEADS, tq, LANES), F32),
            pltpu.VMEM((FOX_HEADS, tq, LANES), F32),
        ],
        compiler_params=_cparams(("arbitrary", "arbitrary")),
        name="fox",
    )(u3, u3, u3, cum5, gain2)


def _hgrn2_kernel(q_ref, i_ref, g_ref, f_ref, lbl_ref, gain_ref, o_ref, b_scr, q_scr, k_scr, st_scr):
    S = q_ref.shape[1]
    n_sub = SUPER // SUB
    a = lbl_ref[...]
    am = jnp.max(a, axis=0, keepdims=True)
    ea = jnp.exp(a - am)
    lb = ea[0:1, :] / (ea[0:1, :] + ea[1:2, :])

    r_i = lax.broadcasted_iota(jnp.int32, (SUPER, SUPER), 0)
    c_i = lax.broadcasted_iota(jnp.int32, (SUPER, SUPER), 1)
    same = (r_i // SUB) == (c_i // SUB)
    tri = (same & (c_i <= r_i)).astype(BF16)
    blk1 = same.astype(BF16)
    ones = jnp.ones((LANES, LANES), BF16)
    lane = lax.broadcasted_iota(jnp.int32, (1, LANES), 1)
    trow = lax.broadcasted_iota(jnp.int32, (8, LANES), 0)

    st_scr[...] = jnp.zeros_like(st_scr)

    def superblock(sb, _):
        r0 = pl.multiple_of(sb * SUPER, SUPER)
        qr = q_ref[0, pl.ds(r0, SUPER), :].astype(F32)
        q = qr * jax.nn.sigmoid(qr)
        f = lb + (1.0 - lb) * jax.nn.sigmoid(f_ref[0, pl.ds(r0, SUPER), :])
        logf = jnp.log(f)
        k = 1.0 - f
        v = i_ref[0, pl.ds(r0, SUPER), :]
        b = _dot3(tri, logf)
        btot = _dot3(blk1, logf)
        qt = (q * jnp.exp(b)).astype(BF16)
        kh = (k * jnp.exp(btot - b)).astype(BF16)
        gam = jnp.exp(btot)
        b_scr[...] = b
        q_scr[...] = q
        k_scr[...] = k
        vt = v.astype(F32).T

        st = st_scr[...]
        o_inter = []
        for c in range(n_sub):
            o_inter.append(_dot_nt(qt[c * SUB:(c + 1) * SUB, :], st.astype(BF16)))
            vtm = jnp.where((lane // SUB) == c, vt, 0.0).astype(BF16)
            st = gam[c * SUB:c * SUB + 1, :] * st + _dot(vtm, kh)
        st_scr[...] = st

        a_rows = []
        for c in range(n_sub):
            base = c * SUB
            tiles = []
            lo_half = []
            for j in range(SUB):
                bs = b_scr[base + j:base + j + 1, :]
                ks = k_scr[base + j:base + j + 1, :]
                halves = []
                for hf in range(2):
                    if hf == 0 and j >= 8:
                        continue
                    t0 = base + 8 * hf
                    p = (q_scr[t0:t0 + 8, :] * jnp.exp(b_scr[t0:t0 + 8, :] - bs)) * ks
                    if j // 8 == hf:
                        p = jnp.where(trow >= (j % 8), p, 0.0)
                    halves.append(p)
                if j < 8:
                    tiles.append(jnp.concatenate(halves, axis=0))
                else:
                    lo_half.append(halves[0])
            for m in range(4):
                tiles.append(jnp.concatenate([lo_half[2 * m], lo_half[2 * m + 1]], axis=0))
            pc = jnp.concatenate(tiles, axis=0).astype(BF16)
            rc = _dot(pc, ones)
            a_top = jnp.zeros((8, LANES), F32)
            a_bot = jnp.zeros((8, LANES), F32)
            for j in range(8):
                sel = lane == (base + j)
                a_top = jnp.where(sel, rc[16 * j:16 * j + 8, :], a_top)
                a_bot = jnp.where(sel, rc[16 * j + 8:16 * j + 16, :], a_bot)
            for j in range(8, SUB):
                sel = lane == (base + j)
                a_bot = jnp.where(sel, rc[128 + 8 * (j - 8):128 + 8 * (j - 8) + 8, :], a_bot)
            a_rows += [a_top, a_bot]
        a_blk = jnp.concatenate(a_rows, axis=0).astype(BF16)
        o = jnp.concatenate(o_inter, axis=0) + _dot(a_blk, v)
        ms = jnp.mean(o * o, axis=-1, keepdims=True)
        o = o * lax.rsqrt(ms + NORM_EPS) * gain_ref[...]
        gr = g_ref[0, pl.ds(r0, SUPER), :].astype(F32)
        o_ref[0, pl.ds(r0, SUPER), :] = (o * (gr * jax.nn.sigmoid(gr))).astype(BF16)
        return 0

    lax.fori_loop(0, S // SUPER, superblock, 0)


def _hgrn2(u3, hgf3, lb_logits, gain):
    B, S, _ = u3.shape
    return pl.pallas_call(
        _hgrn2_kernel,
        grid=(B, HG_HEADS),
        in_specs=[
            pl.BlockSpec((1, S, LANES), lambda b, h: (b, 0, h)),
            pl.BlockSpec((1, S, LANES), lambda b, h: (b, 0, 4 + h)),
            pl.BlockSpec((1, S, LANES), lambda b, h: (b, 0, 8 + h)),
            pl.BlockSpec((1, S, LANES), lambda b, h: (b, 0, h)),
            pl.BlockSpec((2, LANES), lambda b, h: (0, h)),
            pl.BlockSpec((1, LANES), lambda b, h: (0, 0)),
        ],
        out_specs=pl.BlockSpec((1, S, LANES), lambda b, h: (b, 0, h)),
        out_shape=jax.ShapeDtypeStruct((B, S, HG_HEADS * HG_DK), BF16),
        scratch_shapes=[pltpu.VMEM((SUPER, LANES), F32)] * 4,
        compiler_params=_cparams(("arbitrary", "arbitrary")),
        name="hgrn2",
    )(u3, u3, u3, hgf3, lb_logits, gain)


def _hgrn2w_kernel(q_ref, i_ref, g_ref, f_ref, lbl_ref, gain_ref, o_ref, b_scr, q_scr, c_scr, st_scr):
    S, W = q_ref.shape[1], q_ref.shape[2]
    nh = W // LANES
    rows_step = HG_STEP * SUPER
    n_sub = SUPER // SUB
    hsl = [slice(h * LANES, (h + 1) * LANES) for h in range(nh)]
    a = lbl_ref[...]
    am = jnp.max(a, axis=0, keepdims=True)
    ea = jnp.exp(a - am)
    lb = ea[0:1, :] / (ea[0:1, :] + ea[1:2, :])

    r_i = lax.broadcasted_iota(jnp.int32, (SUPER, SUPER), 0)
    c_i = lax.broadcasted_iota(jnp.int32, (SUPER, SUPER), 1)
    same = (r_i // SUB) == (c_i // SUB)
    tri = (same & (c_i <= r_i)).astype(BF16)
    blk1 = same.astype(BF16)
    ones = jnp.ones((LANES, LANES), BF16)
    lane = lax.broadcasted_iota(jnp.int32, (1, LANES), 1)
    trow = lax.broadcasted_iota(jnp.int32, (8, W), 0)

    st_scr[...] = jnp.zeros_like(st_scr)

    def superblock(sb, _):
        r0 = pl.multiple_of(sb * rows_step, rows_step)
        qr = q_ref[0, pl.ds(r0, rows_step), :].astype(F32)
        q = qr * jax.nn.sigmoid(qr)
        f = lb + (1.0 - lb) * jax.nn.sigmoid(f_ref[0, pl.ds(r0, rows_step), :])
        lf2 = jnp.log2(f)
        halves_r = [slice(k * SUPER, (k + 1) * SUPER) for k in range(HG_STEP)]
        b2 = jnp.concatenate([_dot3(tri, lf2[hr]) for hr in halves_r], axis=0)
        bt2 = jnp.concatenate([_dot3(blk1, lf2[hr]) for hr in halves_r], axis=0)
        c2 = b2 - jnp.log2(1.0 - f)
        qt = (q * jnp.exp2(b2)).astype(BF16)
        kh = jnp.exp2(bt2 - c2).astype(BF16)
        gam = jnp.exp2(bt2)
        b_scr[...] = b2
        q_scr[...] = q
        c_scr[...] = c2
        v = i_ref[0, pl.ds(r0, rows_step), :]

        st = [st_scr[h] for h in range(nh)]
        o_inter = [[] for _ in range(nh)]
        for c in range(HG_STEP * n_sub):
            rows = slice(c * SUB, (c + 1) * SUB)
            for h in range(nh):
                o_inter[h].append(_dot_nt(qt[rows, hsl[h]], st[h].astype(BF16)))
                dst = lax.dot_general(v[rows, hsl[h]], kh[rows, hsl[h]], (((0,), (0,)), ((), ())),
                                      preferred_element_type=F32)
                st[h] = gam[c * SUB:c * SUB + 1, hsl[h]] * st[h] + dst
        for h in range(nh):
            st_scr[h] = st[h]

        a_rows = [[] for _ in range(nh)]
        for c in range(HG_STEP * n_sub):
            base = c * SUB
            col0 = base % SUPER
            tiles = []
            lo_half = []
            for j in range(SUB):
                cs = c_scr[base + j:base + j + 1, :]
                halves = []
                for hf in range(2):
                    if hf == 0 and j >= 8:
                        continue
                    t0 = base + 8 * hf
                    p = q_scr[t0:t0 + 8, :] * jnp.exp2(b_scr[t0:t0 + 8, :] - cs)
                    if j // 8 == hf:
                        p = jnp.where(trow >= (j % 8), p, 0.0)
                    halves.append(p)
                if j < 8:
                    tiles.append(jnp.concatenate(halves, axis=0))
                else:
                    lo_half.append(halves[0])
            for m in range(4):
                tiles.append(jnp.concatenate([lo_half[2 * m], lo_half[2 * m + 1]], axis=0))
            pc = jnp.concatenate(tiles, axis=0).astype(BF16)
            for h in range(nh):
                rc = _dot(pc[:, hsl[h]], ones)
                a_top = jnp.zeros((8, LANES), F32)
                a_bot = jnp.zeros((8, LANES), F32)
                for j in range(8):
                    sel = lane == (col0 + j)
                    a_top = jnp.where(sel, rc[16 * j:16 * j + 8, :], a_top)
                    a_bot = jnp.where(sel, rc[16 * j + 8:16 * j + 16, :], a_bot)
                for j in range(8, SUB):
                    sel = lane == (col0 + j)
                    a_bot = jnp.where(sel, rc[128 + 8 * (j - 8):128 + 8 * (j - 8) + 8, :], a_bot)
                a_rows[h] += [a_top, a_bot]

        outs = []
        for h in range(nh):
            parts = []
            for k, hr in enumerate(halves_r):
                tl = a_rows[h][k * 2 * n_sub:(k + 1) * 2 * n_sub]
                a_blk = jnp.concatenate(tl, axis=0).astype(BF16)
                parts.append(_dot(a_blk, v[hr, hsl[h]]))
            o = jnp.concatenate(o_inter[h], axis=0) + jnp.concatenate(parts, axis=0)
            ms = jnp.mean(o * o, axis=-1, keepdims=True)
            outs.append(o * lax.rsqrt(ms + NORM_EPS))
        o = jnp.concatenate(outs, axis=1) * gain_ref[...]
        gr = g_ref[0, pl.ds(r0, rows_step), :].astype(F32)
        o_ref[0, pl.ds(r0, rows_step), :] = (o * (gr * jax.nn.sigmoid(gr))).astype(BF16)
        return 0

    lax.fori_loop(0, S // rows_step, superblock, 0)


def _hgrn2w(u3, hgf3, lb_logits, gain_w):
    B, S, _ = u3.shape
    W = HG_HEADS * HG_DK
    return pl.pallas_call(
        _hgrn2w_kernel,
        grid=(B,),
        in_specs=[
            pl.BlockSpec((1, S, W), lambda b: (b, 0, 0)),
            pl.BlockSpec((1, S, W), lambda b: (b, 0, 1)),
            pl.BlockSpec((1, S, W), lambda b: (b, 0, 2)),
            pl.BlockSpec((1, S, W), lambda b: (b, 0, 0)),
            pl.BlockSpec((2, W), lambda b: (0, 0)),
            pl.BlockSpec((1, W), lambda b: (0, 0)),
        ],
        out_specs=pl.BlockSpec((1, S, W), lambda b: (b, 0, 0)),
        out_shape=jax.ShapeDtypeStruct((B, S, W), BF16),
        scratch_shapes=[pltpu.VMEM((HG_STEP * SUPER, W), F32)] * 3 + [pltpu.VMEM((HG_HEADS, LANES, LANES), F32)],
        compiler_params=_cparams(("arbitrary",)),
        name="hgrn2",
    )(u3, u3, u3, hgf3, lb_logits, gain_w)


ROW_SUB = 8


def _rows_to_tiles(ref, x):
    m = x.shape[0]
    for c in range(ROW_SUB):
        ref[pl.ds(c, m, stride=ROW_SUB), :] = x[:, c * LANES:(c + 1) * LANES]


def _tiles_to_rows(ref):
    m = ref.shape[0] // ROW_SUB
    return jnp.concatenate([ref[pl.ds(c, m, stride=ROW_SUB), :] for c in range(ROW_SUB)], axis=1)


def _row_tile(ref, r):
    return ref.at[pl.ds(pl.multiple_of(r * ROW_SUB, ROW_SUB), ROW_SUB)]


def _outproj_kernel(oa_ref, ob_ref, x_ref, wo_ref, g_ref, wr_ref, br_ref, h_ref, hn_ref, rt_ref, r8_ref):
    half = oa_ref.shape[1]
    h = x_ref[...] + _dot(oa_ref[...], wo_ref[0:half, :]) + _dot(ob_ref[...], wo_ref[half:2 * half, :])
    h_ref[...] = h
    ms = jnp.mean(h * h, axis=-1, keepdims=True)
    hn = (h * lax.rsqrt(ms + NORM_EPS)) * g_ref[...]
    _rows_to_tiles(hn_ref, hn)
    hn_hi = hn.astype(BF16)
    hn_lo = (hn - hn_hi.astype(F32)).astype(BF16)
    wr = wr_ref[...]
    wr_hi = wr.astype(BF16)
    wr_lo = (wr - wr_hi.astype(F32)).astype(BF16)
    logits = _dot(hn_hi, wr_hi) + _dot(hn_lo, wr_hi) + _dot(hn_hi, wr_lo) + br_ref[...]
    lt = logits.T
    g = [lt[i:i + 1, :] for i in range(N_GROUPS)]
    gm = jnp.maximum(jnp.maximum(g[0], g[1]), jnp.maximum(g[2], g[3]))
    gsel = jnp.where(g[0] == gm, 0, jnp.where(g[1] == gm, 1, jnp.where(g[2] == gm, 2, 3)))
    pg = 1.0 / (jnp.exp(g[0] - gm) + jnp.exp(g[1] - gm) + jnp.exp(g[2] - gm) + jnp.exp(g[3] - gm))
    e = []
    for i in range(PER_GROUP):
        rows = [lt[N_GROUPS + PER_GROUP * gg + i:N_GROUPS + PER_GROUP * gg + i + 1, :] for gg in range(N_GROUPS)]
        e.append(jnp.where(gsel == 0, rows[0], jnp.where(gsel == 1, rows[1], jnp.where(gsel == 2, rows[2], rows[3]))))
    e1 = jnp.maximum(jnp.maximum(e[0], e[1]), jnp.maximum(e[2], e[3]))
    i1 = jnp.where(e[0] == e1, 0, jnp.where(e[1] == e1, 1, jnp.where(e[2] == e1, 2, 3)))
    ex = [jnp.where(i1 == i, -jnp.inf, e[i]) for i in range(PER_GROUP)]
    e2 = jnp.maximum(jnp.maximum(ex[0], ex[1]), jnp.maximum(ex[2], ex[3]))
    i2 = jnp.where(ex[0] == e2, 0, jnp.where(ex[1] == e2, 1, jnp.where(ex[2] == e2, 2, 3)))
    r = jnp.exp(e2 - e1)
    w1 = pg / (1.0 + r)
    w2 = w1 * r
    x1 = gsel * PER_GROUP + i1
    x2 = gsel * PER_GROUP + i2
    ridx = lax.broadcasted_iota(jnp.int32, (LANES, 1), 0)
    wt = jnp.where(ridx == 0, w1, 0.0) + jnp.where(ridx == 1, w2, 0.0)
    rt_ref[...] = wt.T
    sidx = lax.broadcasted_iota(jnp.int32, (8, 1), 0)
    r8_ref[...] = (jnp.where(sidx == 0, x1, 0) + jnp.where(sidx == 1, x2, 0)).astype(F32)


def _outproj(oa, ob, x2, w_out, ffn_norm, wr, br, tm):
    T, D = x2.shape
    half = oa.shape[1]
    return pl.pallas_call(
        _outproj_kernel,
        grid=(T // tm,),
        in_specs=[
            pl.BlockSpec((tm, half), lambda i: (i, 0)),
            pl.BlockSpec((tm, half), lambda i: (i, 0)),
            pl.BlockSpec((tm, D), lambda i: (i, 0)),
            pl.BlockSpec(w_out.shape, lambda i: (0, 0)),
            pl.BlockSpec((1, D), lambda i: (0, 0)),
            pl.BlockSpec(wr.shape, lambda i: (0, 0)),
            pl.BlockSpec((1, LANES), lambda i: (0, 0)),
        ],
        out_specs=[
            pl.BlockSpec((tm, D), lambda i: (i, 0)),
            pl.BlockSpec((tm * ROW_SUB, LANES), lambda i: (i, 0)),
            pl.BlockSpec((tm, LANES), lambda i: (i, 0)),
            pl.BlockSpec((8, tm), lambda i: (0, i)),
        ],
        out_shape=[
            jax.ShapeDtypeStruct((T, D), F32),
            jax.ShapeDtypeStruct((T * ROW_SUB, LANES), F32),
            jax.ShapeDtypeStruct((T, LANES), F32),
            jax.ShapeDtypeStruct((8, T), F32),
        ],
        compiler_params=_cparams(("arbitrary",)),
        name="outproj",
    )(oa, ob, x2, w_out, ffn_norm, wr, br)


def _rank_kernel(r8_ref, rank_ref, cnt_ref, off_scr):
    blk = r8_ref.shape[1]

    @pl.when(pl.program_id(0) == 0)
    def _():
        off_scr[...] = jnp.zeros_like(off_scr)

    x1, x2 = r8_ref[0:1, :], r8_ref[1:2, :]
    eid = lax.broadcasted_iota(jnp.int32, (N_EXPERTS, 1), 0).astype(F32)
    r_i = lax.broadcasted_iota(jnp.int32, (blk, blk), 0)
    c_i = lax.broadcasted_iota(jnp.int32, (blk, blk), 1)
    upper = (r_i <= c_i).astype(BF16)
    hit = ((x1 == eid) | (x2 == eid)).astype(BF16)
    cum = _dot(hit, upper) + off_scr[...]
    rank1 = jnp.sum(jnp.where(x1 == eid, cum, 0.0), axis=0, keepdims=True) - 1.0
    rank2 = jnp.sum(jnp.where(x2 == eid, cum, 0.0), axis=0, keepdims=True) - 1.0
    sidx = lax.broadcasted_iota(jnp.int32, (8, 1), 0)
    rank_ref[...] = jnp.where(sidx == 0, rank1, 0.0) + jnp.where(sidx == 1, rank2, 0.0)
    off_scr[...] = cum[:, blk - 1:blk]
    cnt_ref[...] = jnp.broadcast_to(cum[:, blk - 1:blk], cnt_ref.shape)


def _rank(r8, blk):
    T = r8.shape[1]
    return pl.pallas_call(
        _rank_kernel,
        grid=(T // blk,),
        in_specs=[pl.BlockSpec((8, blk), lambda j: (0, j))],
        out_specs=[pl.BlockSpec((8, blk), lambda j: (0, j)), pl.BlockSpec((N_EXPERTS, LANES), lambda j: (0, 0))],
        out_shape=[jax.ShapeDtypeStruct((8, T), F32), jax.ShapeDtypeStruct((N_EXPERTS, LANES), F32)],
        scratch_shapes=[pltpu.VMEM((N_EXPERTS, 1), F32)],
        compiler_params=_cparams(("arbitrary",)),
        name="moe_rank",
    )(r8)


def _dest_kernel(r8_ref, rank_ref, cnt_ref, dest_ref, meta_ref, te_ref, *, tile):
    cnt = cnt_ref[...]
    cntp = jnp.floor((cnt + (tile - 1.0)) * (1.0 / tile)) * tile
    r_i = lax.broadcasted_iota(jnp.int32, (N_EXPERTS, N_EXPERTS), 0)
    c_i = lax.broadcasted_iota(jnp.int32, (N_EXPERTS, N_EXPERTS), 1)
    off = _dot3((c_i < r_i).astype(BF16), cntp)
    offe = off + cntp
    x1, x2 = r8_ref[0:1, :], r8_ref[1:2, :]
    eid = lax.broadcasted_iota(jnp.int32, (N_EXPERTS, 1), 0).astype(F32)
    off_c = off[:, 0:1]
    d1 = jnp.sum(jnp.where(x1 == eid, off_c, 0.0), axis=0, keepdims=True) + rank_ref[0:1, :]
    d2 = jnp.sum(jnp.where(x2 == eid, off_c, 0.0), axis=0, keepdims=True) + rank_ref[1:2, :]
    sidx = lax.broadcasted_iota(jnp.int32, (8, 1), 0)
    dest_ref[...] = (jnp.where(sidx == 0, d1, 0.0) + jnp.where(sidx == 1, d2, 0.0)).astype(jnp.int32)
    lane = lax.broadcasted_iota(jnp.int32, (1, LANES), 1)
    meta_ref[...] = jnp.where(lane == 0, off, offe).astype(jnp.int32)
    start = lax.broadcasted_iota(jnp.int32, (1, te_ref.shape[1]), 1).astype(F32) * tile
    te = jnp.sum((start >= offe[:, 0:1]).astype(F32), axis=0, keepdims=True)
    nvalid = offe[N_EXPERTS - 1:N_EXPERTS, 0:1] * (1.0 / tile)
    te_ref[...] = (jnp.where(sidx == 0, jnp.minimum(te, N_EXPERTS - 1.0), 0.0)
                   + jnp.where(sidx == 1, nvalid, 0.0)).astype(jnp.int32)


def _dest(r8, rank, cnt, blk, tile, n_tiles_pad):
    T = r8.shape[1]
    return pl.pallas_call(
        functools.partial(_dest_kernel, tile=float(tile)),
        grid=(T // blk,),
        in_specs=[pl.BlockSpec((8, blk), lambda j: (0, j)), pl.BlockSpec((8, blk), lambda j: (0, j)),
                  pl.BlockSpec((N_EXPERTS, LANES), lambda j: (0, 0))],
        out_specs=[pl.BlockSpec((8, blk), lambda j: (0, j)), pl.BlockSpec((N_EXPERTS, LANES), lambda j: (0, 0)),
                   pl.BlockSpec((8, n_tiles_pad), lambda j: (0, 0))],
        out_shape=[jax.ShapeDtypeStruct((8, T), jnp.int32), jax.ShapeDtypeStruct((N_EXPERTS, LANES), jnp.int32),
                   jax.ShapeDtypeStruct((8, n_tiles_pad), jnp.int32)],
        compiler_params=_cparams(("arbitrary",)),
        name="moe_dest",
    )(r8, rank, cnt)


def _dispatch_kernel(off_ref, offe_ref, dest_ref, hn_ref, xs_hbm, zero_scr, sem, zsem, *, tm, tile):
    i = pl.program_id(0)

    @pl.when(i == 0)
    def _():
        zero_scr[...] = jnp.zeros_like(zero_scr)
        for e in range(N_EXPERTS):
            @pl.when(offe_ref[e] > off_ref[e])
            def _():
                dst = xs_hbm.at[pl.ds(pl.multiple_of((offe_ref[e] - tile) * ROW_SUB, ROW_SUB), tile * ROW_SUB)]
                c = pltpu.make_async_copy(zero_scr, dst, zsem)
                c.start()
                c.wait()

        def fill_unused(j, c):
            dst = xs_hbm.at[pl.ds(pl.multiple_of(j * tile * ROW_SUB, ROW_SUB), tile * ROW_SUB)]
            cp = pltpu.make_async_copy(zero_scr, dst, zsem)
            cp.start()
            cp.wait()
            return c

        lax.fori_loop(offe_ref[N_EXPERTS - 1] // tile, xs_hbm.shape[0] // (tile * ROW_SUB), fill_unused, 0)

    def issue(r, c):
        for s in range(2):
            pltpu.make_async_copy(_row_tile(hn_ref, r), _row_tile(xs_hbm, dest_ref[0, s, r]), sem).start(priority=s)
        return c

    lax.fori_loop(0, tm, issue, 0, unroll=8)

    def drain(r, c):
        for s in range(2):
            pltpu.make_async_copy(_row_tile(hn_ref, 0), _row_tile(xs_hbm, 0), sem).wait()
        return c

    lax.fori_loop(0, tm, drain, 0, unroll=8)


def _dispatch(off, offe, dest3, hn3, n_rows, tile):
    nt, _, tm = dest3.shape
    return pl.pallas_call(
        functools.partial(_dispatch_kernel, tm=tm, tile=tile),
        grid_spec=pltpu.PrefetchScalarGridSpec(
            num_scalar_prefetch=2,
            grid=(nt,),
            in_specs=[pl.BlockSpec((1, 2, tm), lambda i, o, oe: (i, 0, 0), memory_space=pltpu.SMEM),
                      pl.BlockSpec((tm * ROW_SUB, LANES), lambda i, o, oe: (i, 0))],
            out_specs=pl.BlockSpec(memory_space=pl.ANY),
            scratch_shapes=[pltpu.VMEM((tile * ROW_SUB, LANES), F32), pltpu.SemaphoreType.DMA,
                            pltpu.SemaphoreType.DMA],
        ),
        out_shape=jax.ShapeDtypeStruct((n_rows * ROW_SUB, LANES), F32),
        compiler_params=_cparams(("arbitrary",)),
        name="moe_dispatch",
    )(off, offe, dest3, hn3)


def _experts_kernel(te_ref, nv_ref, xs_ref, wg_ref, wu_ref, wd_ref, ys_ref):
    @pl.when(pl.program_id(0) < nv_ref[0])
    def _():
        n_part = 2
        part = xs_ref.shape[0] // n_part
        rows = [pl.ds(k * part, part) for k in range(n_part)]
        x = [_tiles_to_rows(xs_ref.at[rows[k]]).astype(BF16) for k in range(n_part)]
        a = [_dot(x[k], wg_ref[0]) for k in range(n_part)]
        u = [_dot(x[k], wu_ref[0]) for k in range(n_part)]
        hid = [((a[k] * jax.nn.sigmoid(a[k])) * u[k]).astype(BF16) for k in range(n_part)]
        y = [_dot(hid[k], wd_ref[0]) for k in range(n_part)]
        for k in range(n_part):
            _rows_to_tiles(ys_ref.at[rows[k]], y[k])

    @pl.when(pl.program_id(0) >= nv_ref[0])
    def _():
        ys_ref[...] = jnp.zeros_like(ys_ref)


def _experts(te, nv, xs, wg, wu, wd, tile):
    E, D, H = wg.shape
    blk = (tile * ROW_SUB, LANES)

    def row_map(j, te, nv):
        return (j, 0)

    return pl.pallas_call(
        _experts_kernel,
        grid_spec=pltpu.PrefetchScalarGridSpec(
            num_scalar_prefetch=2,
            grid=(xs.shape[0] // blk[0],),
            in_specs=[pl.BlockSpec(blk, row_map),
                      pl.BlockSpec((1, D, H), lambda j, te, nv: (te[j], 0, 0)),
                      pl.BlockSpec((1, D, H), lambda j, te, nv: (te[j], 0, 0)),
                      pl.BlockSpec((1, H, D), lambda j, te, nv: (te[j], 0, 0))],
            out_specs=pl.BlockSpec(blk, row_map),
        ),
        out_shape=jax.ShapeDtypeStruct(xs.shape, F32),
        compiler_params=_cparams(("arbitrary",)),
        name="moe_experts",
    )(te, nv, xs, wg, wu, wd)


def _combine_kernel(dcur_ref, dnxt_ref, ys_hbm, h_ref, rt_ref, fin_ref, o_ref, buf, sem, *, tm):
    i = pl.program_id(0)
    n = pl.num_programs(0)
    slot = i % 2

    def issue(d_ref, sl):
        def body(r, c):
            for s in range(2):
                pltpu.make_async_copy(_row_tile(ys_hbm, d_ref[0, s, r]), _row_tile(buf.at[sl, s], r),
                                      sem.at[sl]).start(priority=s)
            return c
        lax.fori_loop(0, tm, body, 0, unroll=8)

    @pl.when(i == 0)
    def _():
        issue(dcur_ref, 0)

    @pl.when(i + 1 < n)
    def _():
        issue(dnxt_ref, 1 - slot)

    def drain(r, c):
        for s in range(2):
            pltpu.make_async_copy(_row_tile(ys_hbm, 0), _row_tile(buf.at[slot, s], 0), sem.at[slot]).wait()
        return c

    lax.fori_loop(0, tm, drain, 0, unroll=8)
    y = (h_ref[...] + rt_ref[:, 0:1] * _tiles_to_rows(buf.at[slot, 0])
         + rt_ref[:, 1:2] * _tiles_to_rows(buf.at[slot, 1]))
    ms = jnp.mean(y * y, axis=-1, keepdims=True)
    o_ref[...] = (y * lax.rsqrt(ms + NORM_EPS)) * fin_ref[...]


def _combine(dest3, ys, h, rt, final_norm):
    nt, _, tm = dest3.shape
    T, D = h.shape
    return pl.pallas_call(
        functools.partial(_combine_kernel, tm=tm),
        grid=(nt,),
        in_specs=[pl.BlockSpec((1, 2, tm), lambda i: (i, 0, 0), memory_space=pltpu.SMEM),
                  pl.BlockSpec((1, 2, tm), lambda i: (jnp.minimum(i + 1, nt - 1), 0, 0), memory_space=pltpu.SMEM),
                  pl.BlockSpec(memory_space=pl.ANY),
                  pl.BlockSpec((tm, D), lambda i: (i, 0)),
                  pl.BlockSpec((tm, LANES), lambda i: (i, 0)),
                  pl.BlockSpec((1, D), lambda i: (0, 0))],
        out_specs=pl.BlockSpec((tm, D), lambda i: (i, 0)),
        out_shape=jax.ShapeDtypeStruct((T, D), F32),
        scratch_shapes=[pltpu.VMEM((2, 2, tm * ROW_SUB, LANES), F32), pltpu.SemaphoreType.DMA((2,))],
        compiler_params=_cparams(("arbitrary",)),
        name="moe_combine",
    )(dest3, dest3, ys, h, rt, final_norm)


def _moe_kernel(hn_ref, h_ref, gates_ref, wg_ref, wu_ref, wd_ref, fin_ref, o_ref, acc_ref):
    e = pl.program_id(1)

    @pl.when(e == 0)
    def _():
        acc_ref[...] = h_ref[...]

    lane = lax.broadcasted_iota(jnp.int32, (1, LANES), 1)
    gcol = jnp.sum(jnp.where(lane == e, gates_ref[...], 0.0), axis=-1, keepdims=True)
    t = hn_ref[...]
    a = _dot(t, wg_ref[0].astype(BF16))
    u = _dot(t, wu_ref[0].astype(BF16))
    hid = (a * jax.nn.sigmoid(a)) * u * gcol
    acc_ref[...] += _dot(hid.astype(BF16), wd_ref[0].astype(BF16))

    @pl.when(e == pl.num_programs(1) - 1)
    def _():
        y = acc_ref[...]
        ms = jnp.mean(y * y, axis=-1, keepdims=True)
        o_ref[...] = (y * lax.rsqrt(ms + NORM_EPS)) * fin_ref[...]


def _moe(hn, h, gates, w_gate, w_up, w_down, final_norm, tm):
    T, D = h.shape
    E, _, H = w_gate.shape
    return pl.pallas_call(
        _moe_kernel,
        grid=(T // tm, E),
        in_specs=[
            pl.BlockSpec((tm, D), lambda i, e: (i, 0)),
            pl.BlockSpec((tm, D), lambda i, e: (i, 0)),
            pl.BlockSpec((tm, LANES), lambda i, e: (i, 0)),
            pl.BlockSpec((1, D, H), lambda i, e: (e, 0, 0)),
            pl.BlockSpec((1, D, H), lambda i, e: (e, 0, 0)),
            pl.BlockSpec((1, H, D), lambda i, e: (e, 0, 0)),
            pl.BlockSpec((1, D), lambda i, e: (0, 0)),
        ],
        out_specs=pl.BlockSpec((tm, D), lambda i, e: (i, 0)),
        out_shape=jax.ShapeDtypeStruct((T, D), F32),
        scratch_shapes=[pltpu.VMEM((tm, D), F32)],
        compiler_params=_cparams(("arbitrary", "arbitrary")),
        name="moe",
    )(hn, h, gates, w_gate, w_up, w_down, final_norm)


def kernel(x, attn_norm, w_in, hg_lb_logits, hg_norm, fox_f_bias, fox_norm, w_out, ffn_norm,
           w_group, b_group, w_expert, b_expert, w_gate, w_up, w_down, final_norm):
    B, S, D = x.shape
    T = B * S
    assert w_in.shape[0] == 1, "single-layer block"
    hw = HG_HEADS * HG_DK
    fw = FOX_HEADS * FOX_DH
    wi = w_in[0]
    o = [0, hw, 2 * hw, 3 * hw, 4 * hw, 4 * hw + fw, 4 * hw + 2 * fw, 4 * hw + 3 * fw]
    w_all = jnp.concatenate(
        [wi[:, o[0]:o[1]], wi[:, o[2]:o[3]], wi[:, o[3]:o[4]], wi[:, o[4]:o[7]], wi[:, o[1]:o[2]],
         jnp.pad(wi[:, o[7]:], ((0, 0), (0, LANES - FOX_HEADS)))], axis=1).astype(BF16)

    x2 = x.reshape(T, D)
    tm_in = min(512, S)
    u, hgf, foxf = _inproj(x2, attn_norm.reshape(1, D), w_all, B, S, tm_in)

    cum = _foxcum(foxf.reshape(B * FOX_HEADS, S), jnp.tile(fox_f_bias[0], B).reshape(B * FOX_HEADS, 1),
                  min(256, S))
    tq = min(256, S)
    u3 = u.reshape(B, S, -1)
    o_b = _fox(u3, cum.reshape(B, FOX_HEADS // 2, 2, S // tq, tq),
               jnp.tile(fox_norm[0], 2).reshape(1, LANES), tq)
    o_a = _hgrn2w(u3, hgf.reshape(B, S, hw), hg_lb_logits, jnp.tile(hg_norm[0], HG_HEADS).reshape(1, hw))

    wr = jnp.pad(jnp.concatenate([w_group[0], w_expert[0]], axis=1),
                 ((0, 0), (0, LANES - N_GROUPS - N_EXPERTS)))
    br = jnp.pad(jnp.concatenate([b_group[0], b_expert[0]]), (0, LANES - N_GROUPS - N_EXPERTS)).reshape(1, LANES)
    h, hn3, rt, r8 = _outproj(o_a.reshape(T, hw), o_b.reshape(T, fw), x2, w_out[0].astype(BF16),
                              ffn_norm[0].reshape(1, D), wr, br, min(512, T))

    blk = min(1024, T)
    n_rows = 2 * T + N_EXPERTS * MOE_TILE
    n_tiles = n_rows // MOE_TILE
    rank, cnt = _rank(r8, blk)
    dest, meta, te8 = _dest(r8, rank, cnt, blk, MOE_TILE, -(-n_tiles // LANES) * LANES)
    off, offe = meta[:, 0], meta[:, 1]
    te, nv = te8[0, :n_tiles], te8[1, :1]

    def tiles_of(tm):
        return dest[0:2].reshape(2, T // tm, tm).transpose(1, 0, 2)

    xs = _dispatch(off, offe, tiles_of(min(DISP_TM, T)), hn3, n_rows, MOE_TILE)
    ys = _experts(te, nv, xs, w_gate[0].astype(BF16), w_up[0].astype(BF16), w_down[0].astype(BF16), MOE_TILE)
    out = _combine(tiles_of(min(COMB_TM, T)), ys, h, rt, final_norm.reshape(1, D))
    return out.reshape(B, S, D)
```

```python
import functools

import jax
import jax.numpy as jnp
from jax import lax
from jax.experimental import pallas as pl
from jax.experimental.pallas import tpu as pltpu

F32 = jnp.float32
BF16 = jnp.bfloat16

NORM_EPS = 1e-6
NEG_INF = -1e30

LANES = 128
HG_HEADS = 4
HG_DK = 128
FOX_HEADS = 8
FOX_DH = 64
N_GROUPS = 4
PER_GROUP = 4
N_EXPERTS = 16

FOX_STRIP = 32
SUB = 16
SUPER = 128
HG_STEP = 2
MOE_TILE = 512
DISP_TM = 512
COMB_TM = 256
VMEM_LIMIT = 56 * 1024 * 1024


def _cparams(sem):
    return pltpu.CompilerParams(dimension_semantics=sem, vmem_limit_bytes=VMEM_LIMIT)


def _split3(x):
    hi = x.astype(BF16)
    r1 = x - hi.astype(F32)
    mid = r1.astype(BF16)
    lo = (r1 - mid.astype(F32)).astype(BF16)
    return hi, mid, lo


def _dot(a, b):
    return jnp.dot(a, b, preferred_element_type=F32)


def _dot_nt(a, b):
    return lax.dot_general(a, b, (((1,), (1,)), ((), ())), preferred_element_type=F32)


def _dot3(m_bf16, x_f32):
    hi, mid, lo = _split3(x_f32)
    return _dot(m_bf16, hi) + _dot(m_bf16, mid) + _dot(m_bf16, lo)


def _inproj_kernel(x_ref, g_ref, w_ref, u_ref, hgf_ref, foxf_ref, xn_ref, *, n_main, col_chunk):
    x = x_ref[...]
    ms = jnp.mean(x * x, axis=-1, keepdims=True)
    xn_ref[...] = ((x * lax.rsqrt(ms + NORM_EPS)) * g_ref[...]).astype(BF16)
    for j in range(n_main // col_chunk):
        r = _dot(xn_ref[...], w_ref[:, j * col_chunk:(j + 1) * col_chunk])
        u_ref[:, j * col_chunk:(j + 1) * col_chunk] = r.astype(BF16)
    r = _dot(xn_ref[...], w_ref[:, n_main:n_main + 4 * LANES])
    hgf_ref[...] = r
    r = _dot(xn_ref[...], w_ref[:, n_main + 4 * LANES:n_main + 5 * LANES])
    foxf_ref[0] = r.T[0:FOX_HEADS, :]


def _inproj(x2, attn_norm, w_all, B, S, tm):
    T, D = x2.shape
    n_main = 6 * 512
    per_seq = S // tm
    return pl.pallas_call(
        functools.partial(_inproj_kernel, n_main=n_main, col_chunk=512),
        grid=(T // tm,),
        in_specs=[
            pl.BlockSpec((tm, D), lambda i: (i, 0)),
            pl.BlockSpec((1, D), lambda i: (0, 0)),
            pl.BlockSpec(w_all.shape, lambda i: (0, 0)),
        ],
        out_specs=[
            pl.BlockSpec((tm, n_main), lambda i: (i, 0)),
            pl.BlockSpec((tm, 4 * LANES), lambda i: (i, 0)),
            pl.BlockSpec((1, FOX_HEADS, tm), lambda i: (i // per_seq, 0, i % per_seq)),
        ],
        out_shape=[
            jax.ShapeDtypeStruct((T, n_main), BF16),
            jax.ShapeDtypeStruct((T, 4 * LANES), F32),
            jax.ShapeDtypeStruct((B, FOX_HEADS, S), F32),
        ],
        scratch_shapes=[pltpu.VMEM((tm, D), BF16)],
        compiler_params=_cparams(("arbitrary",)),
        name="inproj",
    )(x2, attn_norm, w_all)


def _foxcum_kernel(f_ref, bias_ref, cum_ref, off_ref):
    blk = f_ref.shape[1]

    @pl.when(pl.program_id(0) == 0)
    def _():
        off_ref[...] = jnp.zeros_like(off_ref)

    r_i = lax.broadcasted_iota(jnp.int32, (blk, blk), 0)
    c_i = lax.broadcasted_iota(jnp.int32, (blk, blk), 1)
    upper = (r_i <= c_i).astype(BF16)
    z = f_ref[...] + bias_ref[...]
    logf = jnp.minimum(z, 0.0) - jnp.log(1.0 + jnp.exp(-jnp.abs(z)))
    hi, mid, lo = _split3(logf)
    cum = _dot(hi, upper) + _dot(mid, upper) + _dot(lo, upper) + off_ref[...]
    cum_ref[...] = cum
    off_ref[...] = cum[:, blk - 1:blk]


def _foxcum(foxf2, bias_col, blk):
    R, S = foxf2.shape
    return pl.pallas_call(
        _foxcum_kernel,
        grid=(S // blk,),
        in_specs=[pl.BlockSpec((R, blk), lambda j: (0, j)), pl.BlockSpec((R, 1), lambda j: (0, 0))],
        out_specs=pl.BlockSpec((R, blk), lambda j: (0, j)),
        out_shape=jax.ShapeDtypeStruct((R, S), F32),
        scratch_shapes=[pltpu.VMEM((R, 1), F32)],
        compiler_params=_cparams(("arbitrary",)),
        name="foxcum",
    )(foxf2, bias_col)


def _fox_kernel(q_ref, k_ref, v_ref, cum_ref, gain_ref, o_ref, qh_scr, s_scr, p_scr, m_scr, al_scr, acc_scr, *, tq):
    qi = pl.program_id(1)
    n_pair = FOX_HEADS // 2
    lane = lax.broadcasted_iota(jnp.int32, (1, LANES), 1)
    in_h = [lane < FOX_DH, lane >= FOX_DH]
    for p in range(n_pair):
        q = q_ref[0, :, p * LANES:(p + 1) * LANES] * jnp.asarray(FOX_DH ** -0.5, BF16)
        for h in range(2):
            qh_scr[2 * p + h] = jnp.where(in_h[h], q, jnp.zeros_like(q))
    rs = FOX_STRIP
    row = lax.broadcasted_iota(jnp.int32, (rs, tq), 0)
    col = lax.broadcasted_iota(jnp.int32, (rs, tq), 1)

    m_scr[...] = jnp.full(m_scr.shape, NEG_INF, F32)
    acc_scr[...] = jnp.zeros_like(acc_scr)

    def step(kb, masked):
        r0 = pl.multiple_of(kb * tq, tq)
        for hd in range(FOX_HEADS):
            p = hd // 2
            s_scr[hd] = _dot_nt(qh_scr[hd], k_ref[0, pl.ds(r0, tq), p * LANES:(p + 1) * LANES])
        for hd in range(FOX_HEADS):
            bias = -cum_ref[0, hd // 2, hd % 2, pl.ds(kb, 1), :]
            for st in range(tq // rs):
                rows = slice(st * rs, (st + 1) * rs)
                s = s_scr[hd, rows, :] + bias
                if masked:
                    s = jnp.where(row + st * rs >= col, s, NEG_INF)
                m = m_scr[hd, rows, :]
                m_new = jnp.maximum(m, jnp.max(s, axis=-1, keepdims=True))
                p_scr[hd, rows, :] = jnp.exp(s - jnp.concatenate([m_new] * (tq // LANES), axis=1)).astype(BF16)
                al_scr[hd, rows, :] = jnp.exp(m - m_new)
                m_scr[hd, rows, :] = m_new
        for hd in range(FOX_HEADS):
            p = hd // 2
            v_blk = v_ref[0, pl.ds(r0, tq), p * LANES:(p + 1) * LANES]
            vh = jnp.where(in_h[hd % 2], v_blk, jnp.ones_like(v_blk))
            acc_scr[hd] = acc_scr[hd] * al_scr[hd] + _dot(p_scr[hd], vh)

    def body(kb, c):
        step(kb, False)
        return c

    lax.fori_loop(0, qi, body, 0)
    step(qi, True)
    for p in range(n_pair):
        a0, a1 = acc_scr[2 * p], acc_scr[2 * p + 1]
        o0 = a0 / pltpu.roll(a0, FOX_DH, 1)
        o1 = a1 / pltpu.roll(a1, FOX_DH, 1)
        o = jnp.where(in_h[0], o0, o1)
        o2 = o * o
        ms0 = jnp.sum(jnp.where(in_h[0], o2, 0.0), axis=-1, keepdims=True) * (1.0 / FOX_DH)
        ms1 = jnp.sum(jnp.where(in_h[1], o2, 0.0), axis=-1, keepdims=True) * (1.0 / FOX_DH)
        ms = jnp.where(in_h[0], ms0, ms1)
        o_ref[0, :, p * LANES:(p + 1) * LANES] = (o * lax.rsqrt(ms + NORM_EPS) * gain_ref[...]).astype(BF16)


def _fox(u3, cum5, gain2, tq):
    B, S, _ = u3.shape
    nq = S // tq
    fw = FOX_HEADS * FOX_DH
    qc, kc, vc = 3, 4, 5
    return pl.pallas_call(
        functools.partial(_fox_kernel, tq=tq),
        grid=(B, nq),
        in_specs=[
            pl.BlockSpec((1, tq, fw), lambda b, i: (b, i, qc)),
            pl.BlockSpec((1, S, fw), lambda b, i: (b, 0, kc)),
            pl.BlockSpec((1, S, fw), lambda b, i: (b, 0, vc)),
            pl.BlockSpec((1, FOX_HEADS // 2, 2, nq, tq), lambda b, i: (b, 0, 0, 0, 0)),
            pl.BlockSpec((1, LANES), lambda b, i: (0, 0)),
        ],
        out_specs=pl.BlockSpec((1, tq, fw), lambda b, i: (b, i, 0)),
        out_shape=jax.ShapeDtypeStruct((B, S, fw), BF16),
        scratch_shapes=[
            pltpu.VMEM((FOX_HEADS, tq, LANES), BF16),
            pltpu.VMEM((FOX_HEADS, tq, tq), F32),
            pltpu.VMEM((FOX_HEADS, tq, tq), BF16),
            pltpu.VMEM((FOX_HEADS, tq, LANES), F32),
            pltpu.VMEM((FOX_HEADS, tq, LANES), F32),
            pltpu.VMEM((FOX_HEADS, tq, LANES), F32),
        ],
        compiler_params=_cparams(("arbitrary", "arbitrary")),
        name="fox",
    )(u3, u3, u3, cum5, gain2)


def _foxt_kernel(q_ref, k_ref, v_ref, cum_ref, gain_ref, o_ref,
                 kaug_scr, vt_scr, qa_scr, s_scr, p_scr, m_scr, al_scr, acc_scr, *, tq):
    qi = pl.program_id(1)
    nkb = vt_scr.shape[1]
    lane = lax.broadcasted_iota(jnp.int32, (1, LANES), 1)
    rid = lax.broadcasted_iota(jnp.int32, (LANES, 1), 0)
    lo_lane, lo_row = lane < FOX_DH, rid < FOX_DH
    aug_lane = [(lane >= FOX_DH) & (lane < FOX_DH + 3), lane < 3]

    @pl.when(qi == 0)
    def _():
        for kb in range(nkb):
            rows = slice(kb * tq, (kb + 1) * tq)
            for p in range(FOX_HEADS // 2):
                ls = slice(p * LANES, (p + 1) * LANES)
                pe = _split3(cum_ref[0, p, 0, kb:kb + 1, :])
                po = _split3(cum_ref[0, p, 1, kb:kb + 1, :])
                x = jnp.zeros((LANES, tq), F32)
                for j in range(3):
                    x = jnp.where(rid == FOX_DH + j, pe[j].astype(F32), x)
                    x = jnp.where(rid == j, po[j].astype(F32), x)
                aug = x.T
                kf = k_ref[0, rows, ls].astype(F32)
                kaug_scr[2 * p, rows, :] = jnp.where(lo_lane, kf, aug).astype(BF16)
                kaug_scr[2 * p + 1, rows, :] = jnp.where(lo_lane, aug, kf).astype(BF16)
                vt = v_ref[0, rows, ls].astype(F32).T
                vt_scr[2 * p, kb] = jnp.where(lo_row, vt, 1.0).astype(BF16)
                vt_scr[2 * p + 1, kb] = jnp.where(lo_row, 1.0, vt).astype(BF16)

    for p in range(FOX_HEADS // 2):
        q = q_ref[0, :, p * LANES:(p + 1) * LANES].astype(F32) * (FOX_DH ** -0.5)
        qa_scr[2 * p] = jnp.where(lo_lane, q, jnp.where(aug_lane[0], -1.0, 0.0)).astype(BF16)
        qa_scr[2 * p + 1] = jnp.where(lo_lane, jnp.where(aug_lane[1], -1.0, 0.0), q).astype(BF16)
    m_scr[...] = jnp.full(m_scr.shape, NEG_INF, F32)
    acc_scr[...] = jnp.zeros_like(acc_scr)
    krow = lax.broadcasted_iota(jnp.int32, (tq, LANES), 0)
    qcol = lax.broadcasted_iota(jnp.int32, (tq, LANES), 1)

    def step(kb, masked):
        r0 = pl.multiple_of(kb * tq, tq)
        for hd in range(FOX_HEADS):
            s_scr[hd] = _dot_nt(kaug_scr[hd, pl.ds(r0, tq), :], qa_scr[hd])
        for hd in range(FOX_HEADS):
            for lt in range(tq // LANES):
                ls = slice(lt * LANES, (lt + 1) * LANES)
                s = s_scr[hd, :, ls]
                if masked:
                    s = jnp.where(krow <= qcol + lt * LANES, s, NEG_INF)
                m = m_scr[hd:hd + 1, ls]
                m_new = jnp.maximum(m, jnp.max(s, axis=0, keepdims=True))
                p_scr[hd, :, ls] = jnp.exp(s - m_new).astype(BF16)
                al_scr[hd:hd + 1, ls] = jnp.exp(m - m_new)
                m_scr[hd:hd + 1, ls] = m_new
        for hd in range(FOX_HEADS):
            acc_scr[hd] = acc_scr[hd] * al_scr[hd:hd + 1, :] + _dot(vt_scr[hd, kb], p_scr[hd])

    def body(kb, c):
        step(kb, False)
        return c

    lax.fori_loop(0, qi, body, 0)
    step(qi, True)
    for p in range(FOX_HEADS // 2):
        ae, ao = acc_scr[2 * p], acc_scr[2 * p + 1]
        o = jnp.where(lo_row, ae / ae[FOX_DH:FOX_DH + 1, :], ao / ao[0:1, :])
        o2 = o * o
        ms_e = jnp.sum(jnp.where(lo_row, o2, 0.0), axis=0, keepdims=True) * (1.0 / FOX_DH)
        ms_o = jnp.sum(jnp.where(lo_row, 0.0, o2), axis=0, keepdims=True) * (1.0 / FOX_DH)
        o = o * lax.rsqrt(jnp.where(lo_row, ms_e, ms_o) + NORM_EPS)
        o_ref[0, :, p * LANES:(p + 1) * LANES] = (o.T * gain_ref[...]).astype(BF16)


def _foxt(u3, cum5, gain2, tq):
    B, S, _ = u3.shape
    nq = S // tq
    fw = FOX_HEADS * FOX_DH
    qc, kc, vc = 3, 4, 5
    return pl.pallas_call(
        functools.partial(_foxt_kernel, tq=tq),
        grid=(B, nq),
        in_specs=[
            pl.BlockSpec((1, tq, fw), lambda b, i: (b, i, qc)),
            pl.BlockSpec((1, S, fw), lambda b, i: (b, 0, kc)),
            pl.BlockSpec((1, S, fw), lambda b, i: (b, 0, vc)),
            pl.BlockSpec((1, FOX_HEADS // 2, 2, nq, tq), lambda b, i: (b, 0, 0, 0, 0)),
            pl.BlockSpec((1, LANES), lambda b, i: (0, 0)),
        ],
        out_specs=pl.BlockSpec((1, tq, fw), lambda b, i: (b, i, 0)),
        out_shape=jax.ShapeDtypeStruct((B, S, fw), BF16),
        scratch_shapes=[
            pltpu.VMEM((FOX_HEADS, S, LANES), BF16),
            pltpu.VMEM((FOX_HEADS, nq, LANES, tq), BF16),
            pltpu.VMEM((FOX_HEADS, tq, LANES), BF16),
            pltpu.VMEM((FOX_HEADS, tq, tq), F32),
            pltpu.VMEM((FOX_HEADS, tq, tq), BF16),
            pltpu.VMEM((FOX_HEADS, tq), F32),
            pltpu.VMEM((FOX_HEADS, tq), F32),
            pltpu.VMEM((FOX_HEADS, LANES, tq), F32),
        ],
        compiler_params=_cparams(("arbitrary", "arbitrary")),
        name="fox",
    )(u3, u3, u3, cum5, gain2)


def _hgrn2_kernel(q_ref, i_ref, g_ref, f_ref, lbl_ref, gain_ref, o_ref, b_scr, q_scr, k_scr, st_scr):
    S = q_ref.shape[1]
    n_sub = SUPER // SUB
    a = lbl_ref[...]
    am = jnp.max(a, axis=0, keepdims=True)
    ea = jnp.exp(a - am)
    lb = ea[0:1, :] / (ea[0:1, :] + ea[1:2, :])

    r_i = lax.broadcasted_iota(jnp.int32, (SUPER, SUPER), 0)
    c_i = lax.broadcasted_iota(jnp.int32, (SUPER, SUPER), 1)
    same = (r_i // SUB) == (c_i // SUB)
    tri = (same & (c_i <= r_i)).astype(BF16)
    blk1 = same.astype(BF16)
    ones = jnp.ones((LANES, LANES), BF16)
    lane = lax.broadcasted_iota(jnp.int32, (1, LANES), 1)
    trow = lax.broadcasted_iota(jnp.int32, (8, LANES), 0)

    st_scr[...] = jnp.zeros_like(st_scr)

    def superblock(sb, _):
        r0 = pl.multiple_of(sb * SUPER, SUPER)
        qr = q_ref[0, pl.ds(r0, SUPER), :].astype(F32)
        q = qr * jax.nn.sigmoid(qr)
        f = lb + (1.0 - lb) * jax.nn.sigmoid(f_ref[0, pl.ds(r0, SUPER), :])
        logf = jnp.log(f)
        k = 1.0 - f
        v = i_ref[0, pl.ds(r0, SUPER), :]
        b = _dot3(tri, logf)
        btot = _dot3(blk1, logf)
        qt = (q * jnp.exp(b)).astype(BF16)
        kh = (k * jnp.exp(btot - b)).astype(BF16)
        gam = jnp.exp(btot)
        b_scr[...] = b
        q_scr[...] = q
        k_scr[...] = k
        vt = v.astype(F32).T

        st = st_scr[...]
        o_inter = []
        for c in range(n_sub):
            o_inter.append(_dot_nt(qt[c * SUB:(c + 1) * SUB, :], st.astype(BF16)))
            vtm = jnp.where((lane // SUB) == c, vt, 0.0).astype(BF16)
            st = gam[c * SUB:c * SUB + 1, :] * st + _dot(vtm, kh)
        st_scr[...] = st

        a_rows = []
        for c in range(n_sub):
            base = c * SUB
            tiles = []
            lo_half = []
            for j in range(SUB):
                bs = b_scr[base + j:base + j + 1, :]
                ks = k_scr[base + j:base + j + 1, :]
                halves = []
                for hf in range(2):
                    if hf == 0 and j >= 8:
                        continue
                    t0 = base + 8 * hf
                    p = (q_scr[t0:t0 + 8, :] * jnp.exp(b_scr[t0:t0 + 8, :] - bs)) * ks
                    if j // 8 == hf:
                        p = jnp.where(trow >= (j % 8), p, 0.0)
                    halves.append(p)
                if j < 8:
                    tiles.append(jnp.concatenate(halves, axis=0))
                else:
                    lo_half.append(halves[0])
            for m in range(4):
                tiles.append(jnp.concatenate([lo_half[2 * m], lo_half[2 * m + 1]], axis=0))
            pc = jnp.concatenate(tiles, axis=0).astype(BF16)
            rc = _dot(pc, ones)
            a_top = jnp.zeros((8, LANES), F32)
            a_bot = jnp.zeros((8, LANES), F32)
            for j in range(8):
                sel = lane == (base + j)
                a_top = jnp.where(sel, rc[16 * j:16 * j + 8, :], a_top)
                a_bot = jnp.where(sel, rc[16 * j + 8:16 * j + 16, :], a_bot)
            for j in range(8, SUB):
                sel = lane == (base + j)
                a_bot = jnp.where(sel, rc[128 + 8 * (j - 8):128 + 8 * (j - 8) + 8, :], a_bot)
            a_rows += [a_top, a_bot]
        a_blk = jnp.concatenate(a_rows, axis=0).astype(BF16)
        o = jnp.concatenate(o_inter, axis=0) + _dot(a_blk, v)
        ms = jnp.mean(o * o, axis=-1, keepdims=True)
        o = o * lax.rsqrt(ms + NORM_EPS) * gain_ref[...]
        gr = g_ref[0, pl.ds(r0, SUPER), :].astype(F32)
        o_ref[0, pl.ds(r0, SUPER), :] = (o * (gr * jax.nn.sigmoid(gr))).astype(BF16)
        return 0

    lax.fori_loop(0, S // SUPER, superblock, 0)


def _hgrn2(u3, hgf3, lb_logits, gain):
    B, S, _ = u3.shape
    return pl.pallas_call(
        _hgrn2_kernel,
        grid=(B, HG_HEADS),
        in_specs=[
            pl.BlockSpec((1, S, LANES), lambda b, h: (b, 0, h)),
            pl.BlockSpec((1, S, LANES), lambda b, h: (b, 0, 4 + h)),
            pl.BlockSpec((1, S, LANES), lambda b, h: (b, 0, 8 + h)),
            pl.BlockSpec((1, S, LANES), lambda b, h: (b, 0, h)),
            pl.BlockSpec((2, LANES), lambda b, h: (0, h)),
            pl.BlockSpec((1, LANES), lambda b, h: (0, 0)),
        ],
        out_specs=pl.BlockSpec((1, S, LANES), lambda b, h: (b, 0, h)),
        out_shape=jax.ShapeDtypeStruct((B, S, HG_HEADS * HG_DK), BF16),
        scratch_shapes=[pltpu.VMEM((SUPER, LANES), F32)] * 4,
        compiler_params=_cparams(("arbitrary", "arbitrary")),
        name="hgrn2",
    )(u3, u3, u3, hgf3, lb_logits, gain)


def _hgrn2w_kernel(q_ref, i_ref, g_ref, f_ref, lbl_ref, gain_ref, o_ref, b_scr, q_scr, c_scr, st_scr):
    S, W = q_ref.shape[1], q_ref.shape[2]
    nh = W // LANES
    rows_step = HG_STEP * SUPER
    n_sub = SUPER // SUB
    hsl = [slice(h * LANES, (h + 1) * LANES) for h in range(nh)]
    a = lbl_ref[...]
    am = jnp.max(a, axis=0, keepdims=True)
    ea = jnp.exp(a - am)
    lb = ea[0:1, :] / (ea[0:1, :] + ea[1:2, :])

    r_i = lax.broadcasted_iota(jnp.int32, (SUPER, SUPER), 0)
    c_i = lax.broadcasted_iota(jnp.int32, (SUPER, SUPER), 1)
    same = (r_i // SUB) == (c_i // SUB)
    tri = (same & (c_i <= r_i)).astype(BF16)
    blk1 = same.astype(BF16)
    ones = jnp.ones((LANES, LANES), BF16)
    lane = lax.broadcasted_iota(jnp.int32, (1, LANES), 1)
    trow = lax.broadcasted_iota(jnp.int32, (8, W), 0)

    st_scr[...] = jnp.zeros_like(st_scr)

    def superblock(sb, _):
        r0 = pl.multiple_of(sb * rows_step, rows_step)
        qr = q_ref[0, pl.ds(r0, rows_step), :].astype(F32)
        q = qr * jax.nn.sigmoid(qr)
        f = lb + (1.0 - lb) * jax.nn.sigmoid(f_ref[0, pl.ds(r0, rows_step), :])
        lf2 = jnp.log2(f)
        halves_r = [slice(k * SUPER, (k + 1) * SUPER) for k in range(HG_STEP)]
        b2 = jnp.concatenate([_dot3(tri, lf2[hr]) for hr in halves_r], axis=0)
        bt2 = jnp.concatenate([_dot3(blk1, lf2[hr]) for hr in halves_r], axis=0)
        c2 = b2 - jnp.log2(1.0 - f)
        qt = (q * jnp.exp2(b2)).astype(BF16)
        kh = jnp.exp2(bt2 - c2).astype(BF16)
        gam = jnp.exp2(bt2)
        b_scr[...] = b2
        q_scr[...] = q
        c_scr[...] = c2
        v = i_ref[0, pl.ds(r0, rows_step), :]

        st = [st_scr[h] for h in range(nh)]
        o_inter = [[] for _ in range(nh)]
        for c in range(HG_STEP * n_sub):
            rows = slice(c * SUB, (c + 1) * SUB)
            for h in range(nh):
                o_inter[h].append(_dot_nt(qt[rows, hsl[h]], st[h].astype(BF16)))
                dst = lax.dot_general(v[rows, hsl[h]], kh[rows, hsl[h]], (((0,), (0,)), ((), ())),
                                      preferred_element_type=F32)
                st[h] = gam[c * SUB:c * SUB + 1, hsl[h]] * st[h] + dst
        for h in range(nh):
            st_scr[h] = st[h]

        a_rows = [[] for _ in range(nh)]
        for c in range(HG_STEP * n_sub):
            base = c * SUB
            col0 = base % SUPER
            tiles = []
            lo_half = []
            for j in range(SUB):
                cs = c_scr[base + j:base + j + 1, :]
                halves = []
                for hf in range(2):
                    if hf == 0 and j >= 8:
                        continue
                    t0 = base + 8 * hf
                    p = q_scr[t0:t0 + 8, :] * jnp.exp2(b_scr[t0:t0 + 8, :] - cs)
                    if j // 8 == hf:
                        p = jnp.where(trow >= (j % 8), p, 0.0)
                    halves.append(p)
                if j < 8:
                    tiles.append(jnp.concatenate(halves, axis=0))
                else:
                    lo_half.append(halves[0])
            for m in range(4):
                tiles.append(jnp.concatenate([lo_half[2 * m], lo_half[2 * m + 1]], axis=0))
            pc = jnp.concatenate(tiles, axis=0).astype(BF16)
            for h in range(nh):
                rc = _dot(pc[:, hsl[h]], ones)
                a_top = jnp.zeros((8, LANES), F32)
                a_bot = jnp.zeros((8, LANES), F32)
                for j in range(8):
                    sel = lane == (col0 + j)
                    a_top = jnp.where(sel, rc[16 * j:16 * j + 8, :], a_top)
                    a_bot = jnp.where(sel, rc[16 * j + 8:16 * j + 16, :], a_bot)
                for j in range(8, SUB):
                    sel = lane == (col0 + j)
                    a_bot = jnp.where(sel, rc[128 + 8 * (j - 8):128 + 8 * (j - 8) + 8, :], a_bot)
                a_rows[h] += [a_top, a_bot]

        outs = []
        for h in range(nh):
            parts = []
            for k, hr in enumerate(halves_r):
                tl = a_rows[h][k * 2 * n_sub:(k + 1) * 2 * n_sub]
                a_blk = jnp.concatenate(tl, axis=0).astype(BF16)
                parts.append(_dot(a_blk, v[hr, hsl[h]]))
            o = jnp.concatenate(o_inter[h], axis=0) + jnp.concatenate(parts, axis=0)
            ms = jnp.mean(o * o, axis=-1, keepdims=True)
            outs.append(o * lax.rsqrt(ms + NORM_EPS))
        o = jnp.concatenate(outs, axis=1) * gain_ref[...]
        gr = g_ref[0, pl.ds(r0, rows_step), :].astype(F32)
        o_ref[0, pl.ds(r0, rows_step), :] = (o * (gr * jax.nn.sigmoid(gr))).astype(BF16)
        return 0

    lax.fori_loop(0, S // rows_step, superblock, 0)


def _hgrn2w(u3, hgf3, lb_logits, gain_w):
    B, S, _ = u3.shape
    W = HG_HEADS * HG_DK
    return pl.pallas_call(
        _hgrn2w_kernel,
        grid=(B,),
        in_specs=[
            pl.BlockSpec((1, S, W), lambda b: (b, 0, 0)),
            pl.BlockSpec((1, S, W), lambda b: (b, 0, 1)),
            pl.BlockSpec((1, S, W), lambda b: (b, 0, 2)),
            pl.BlockSpec((1, S, W), lambda b: (b, 0, 0)),
            pl.BlockSpec((2, W), lambda b: (0, 0)),
            pl.BlockSpec((1, W), lambda b: (0, 0)),
        ],
        out_specs=pl.BlockSpec((1, S, W), lambda b: (b, 0, 0)),
        out_shape=jax.ShapeDtypeStruct((B, S, W), BF16),
        scratch_shapes=[pltpu.VMEM((HG_STEP * SUPER, W), F32)] * 3 + [pltpu.VMEM((HG_HEADS, LANES, LANES), F32)],
        compiler_params=_cparams(("arbitrary",)),
        name="hgrn2",
    )(u3, u3, u3, hgf3, lb_logits, gain_w)


ROW_SUB = 8


def _rows_to_tiles(ref, x):
    m = x.shape[0]
    for c in range(ROW_SUB):
        ref[pl.ds(c, m, stride=ROW_SUB), :] = x[:, c * LANES:(c + 1) * LANES]


def _tiles_to_rows(ref):
    m = ref.shape[0] // ROW_SUB
    return jnp.concatenate([ref[pl.ds(c, m, stride=ROW_SUB), :] for c in range(ROW_SUB)], axis=1)


def _row_tile(ref, r):
    return ref.at[pl.ds(pl.multiple_of(r * ROW_SUB, ROW_SUB), ROW_SUB)]


def _outproj_kernel(oa_ref, ob_ref, x_ref, wo_ref, g_ref, wr_ref, br_ref, h_ref, hn_ref, rt_ref, r8_ref):
    half = oa_ref.shape[1]
    h = x_ref[...] + _dot(oa_ref[...], wo_ref[0:half, :]) + _dot(ob_ref[...], wo_ref[half:2 * half, :])
    h_ref[...] = h
    ms = jnp.mean(h * h, axis=-1, keepdims=True)
    hn = (h * lax.rsqrt(ms + NORM_EPS)) * g_ref[...]
    _rows_to_tiles(hn_ref, hn)
    hn_hi = hn.astype(BF16)
    hn_lo = (hn - hn_hi.astype(F32)).astype(BF16)
    wr = wr_ref[...]
    wr_hi = wr.astype(BF16)
    wr_lo = (wr - wr_hi.astype(F32)).astype(BF16)
    logits = _dot(hn_hi, wr_hi) + _dot(hn_lo, wr_hi) + _dot(hn_hi, wr_lo) + br_ref[...]
    lt = logits.T
    g = [lt[i:i + 1, :] for i in range(N_GROUPS)]
    gm = jnp.maximum(jnp.maximum(g[0], g[1]), jnp.maximum(g[2], g[3]))
    gsel = jnp.where(g[0] == gm, 0, jnp.where(g[1] == gm, 1, jnp.where(g[2] == gm, 2, 3)))
    pg = 1.0 / (jnp.exp(g[0] - gm) + jnp.exp(g[1] - gm) + jnp.exp(g[2] - gm) + jnp.exp(g[3] - gm))
    e = []
    for i in range(PER_GROUP):
        rows = [lt[N_GROUPS + PER_GROUP * gg + i:N_GROUPS + PER_GROUP * gg + i + 1, :] for gg in range(N_GROUPS)]
        e.append(jnp.where(gsel == 0, rows[0], jnp.where(gsel == 1, rows[1], jnp.where(gsel == 2, rows[2], rows[3]))))
    e1 = jnp.maximum(jnp.maximum(e[0], e[1]), jnp.maximum(e[2], e[3]))
    i1 = jnp.where(e[0] == e1, 0, jnp.where(e[1] == e1, 1, jnp.where(e[2] == e1, 2, 3)))
    ex = [jnp.where(i1 == i, -jnp.inf, e[i]) for i in range(PER_GROUP)]
    e2 = jnp.maximum(jnp.maximum(ex[0], ex[1]), jnp.maximum(ex[2], ex[3]))
    i2 = jnp.where(ex[0] == e2, 0, jnp.where(ex[1] == e2, 1, jnp.where(ex[2] == e2, 2, 3)))
    r = jnp.exp(e2 - e1)
    w1 = pg / (1.0 + r)
    w2 = w1 * r
    x1 = gsel * PER_GROUP + i1
    x2 = gsel * PER_GROUP + i2
    ridx = lax.broadcasted_iota(jnp.int32, (LANES, 1), 0)
    wt = jnp.where(ridx == 0, w1, 0.0) + jnp.where(ridx == 1, w2, 0.0)
    rt_ref[...] = wt.T
    sidx = lax.broadcasted_iota(jnp.int32, (8, 1), 0)
    r8_ref[...] = (jnp.where(sidx == 0, x1, 0) + jnp.where(sidx == 1, x2, 0)).astype(F32)


def _outproj(oa, ob, x2, w_out, ffn_norm, wr, br, tm):
    T, D = x2.shape
    half = oa.shape[1]
    return pl.pallas_call(
        _outproj_kernel,
        grid=(T // tm,),
        in_specs=[
            pl.BlockSpec((tm, half), lambda i: (i, 0)),
            pl.BlockSpec((tm, half), lambda i: (i, 0)),
            pl.BlockSpec((tm, D), lambda i: (i, 0)),
            pl.BlockSpec(w_out.shape, lambda i: (0, 0)),
            pl.BlockSpec((1, D), lambda i: (0, 0)),
            pl.BlockSpec(wr.shape, lambda i: (0, 0)),
            pl.BlockSpec((1, LANES), lambda i: (0, 0)),
        ],
        out_specs=[
            pl.BlockSpec((tm, D), lambda i: (i, 0)),
            pl.BlockSpec((tm * ROW_SUB, LANES), lambda i: (i, 0)),
            pl.BlockSpec((tm, LANES), lambda i: (i, 0)),
            pl.BlockSpec((8, tm), lambda i: (0, i)),
        ],
        out_shape=[
            jax.ShapeDtypeStruct((T, D), F32),
            jax.ShapeDtypeStruct((T * ROW_SUB, LANES), F32),
            jax.ShapeDtypeStruct((T, LANES), F32),
            jax.ShapeDtypeStruct((8, T), F32),
        ],
        compiler_params=_cparams(("arbitrary",)),
        name="outproj",
    )(oa, ob, x2, w_out, ffn_norm, wr, br)


def _rank_kernel(r8_ref, rank_ref, cnt_ref, off_scr):
    blk = r8_ref.shape[1]

    @pl.when(pl.program_id(0) == 0)
    def _():
        off_scr[...] = jnp.zeros_like(off_scr)

    x1, x2 = r8_ref[0:1, :], r8_ref[1:2, :]
    eid = lax.broadcasted_iota(jnp.int32, (N_EXPERTS, 1), 0).astype(F32)
    r_i = lax.broadcasted_iota(jnp.int32, (blk, blk), 0)
    c_i = lax.broadcasted_iota(jnp.int32, (blk, blk), 1)
    upper = (r_i <= c_i).astype(BF16)
    hit = ((x1 == eid) | (x2 == eid)).astype(BF16)
    cum = _dot(hit, upper) + off_scr[...]
    rank1 = jnp.sum(jnp.where(x1 == eid, cum, 0.0), axis=0, keepdims=True) - 1.0
    rank2 = jnp.sum(jnp.where(x2 == eid, cum, 0.0), axis=0, keepdims=True) - 1.0
    sidx = lax.broadcasted_iota(jnp.int32, (8, 1), 0)
    rank_ref[...] = jnp.where(sidx == 0, rank1, 0.0) + jnp.where(sidx == 1, rank2, 0.0)
    off_scr[...] = cum[:, blk - 1:blk]
    cnt_ref[...] = jnp.broadcast_to(cum[:, blk - 1:blk], cnt_ref.shape)


def _rank(r8, blk):
    T = r8.shape[1]
    return pl.pallas_call(
        _rank_kernel,
        grid=(T // blk,),
        in_specs=[pl.BlockSpec((8, blk), lambda j: (0, j))],
        out_specs=[pl.BlockSpec((8, blk), lambda j: (0, j)), pl.BlockSpec((N_EXPERTS, LANES), lambda j: (0, 0))],
        out_shape=[jax.ShapeDtypeStruct((8, T), F32), jax.ShapeDtypeStruct((N_EXPERTS, LANES), F32)],
        scratch_shapes=[pltpu.VMEM((N_EXPERTS, 1), F32)],
        compiler_params=_cparams(("arbitrary",)),
        name="moe_rank",
    )(r8)


def _dest_kernel(r8_ref, rank_ref, cnt_ref, dest_ref, meta_ref, te_ref, *, tile):
    cnt = cnt_ref[...]
    cntp = jnp.floor((cnt + (tile - 1.0)) * (1.0 / tile)) * tile
    r_i = lax.broadcasted_iota(jnp.int32, (N_EXPERTS, N_EXPERTS), 0)
    c_i = lax.broadcasted_iota(jnp.int32, (N_EXPERTS, N_EXPERTS), 1)
    off = _dot3((c_i < r_i).astype(BF16), cntp)
    offe = off + cntp
    x1, x2 = r8_ref[0:1, :], r8_ref[1:2, :]
    eid = lax.broadcasted_iota(jnp.int32, (N_EXPERTS, 1), 0).astype(F32)
    off_c = off[:, 0:1]
    d1 = jnp.sum(jnp.where(x1 == eid, off_c, 0.0), axis=0, keepdims=True) + rank_ref[0:1, :]
    d2 = jnp.sum(jnp.where(x2 == eid, off_c, 0.0), axis=0, keepdims=True) + rank_ref[1:2, :]
    sidx = lax.broadcasted_iota(jnp.int32, (8, 1), 0)
    dest_ref[...] = (jnp.where(sidx == 0, d1, 0.0) + jnp.where(sidx == 1, d2, 0.0)).astype(jnp.int32)
    lane = lax.broadcasted_iota(jnp.int32, (1, LANES), 1)
    meta_ref[...] = jnp.where(lane == 0, off, offe).astype(jnp.int32)
    start = lax.broadcasted_iota(jnp.int32, (1, te_ref.shape[1]), 1).astype(F32) * tile
    te = jnp.sum((start >= offe[:, 0:1]).astype(F32), axis=0, keepdims=True)
    nvalid = offe[N_EXPERTS - 1:N_EXPERTS, 0:1] * (1.0 / tile)
    te_ref[...] = (jnp.where(sidx == 0, jnp.minimum(te, N_EXPERTS - 1.0), 0.0)
                   + jnp.where(sidx == 1, nvalid, 0.0)).astype(jnp.int32)


def _dest(r8, rank, cnt, blk, tile, n_tiles_pad):
    T = r8.shape[1]
    return pl.pallas_call(
        functools.partial(_dest_kernel, tile=float(tile)),
        grid=(T // blk,),
        in_specs=[pl.BlockSpec((8, blk), lambda j: (0, j)), pl.BlockSpec((8, blk), lambda j: (0, j)),
                  pl.BlockSpec((N_EXPERTS, LANES), lambda j: (0, 0))],
        out_specs=[pl.BlockSpec((8, blk), lambda j: (0, j)), pl.BlockSpec((N_EXPERTS, LANES), lambda j: (0, 0)),
                   pl.BlockSpec((8, n_tiles_pad), lambda j: (0, 0))],
        out_shape=[jax.ShapeDtypeStruct((8, T), jnp.int32), jax.ShapeDtypeStruct((N_EXPERTS, LANES), jnp.int32),
                   jax.ShapeDtypeStruct((8, n_tiles_pad), jnp.int32)],
        compiler_params=_cparams(("arbitrary",)),
        name="moe_dest",
    )(r8, rank, cnt)


def _dispatch_kernel(off_ref, offe_ref, dest_ref, hn_ref, xs_hbm, zero_scr, sem, zsem, *, tm, tile):
    i = pl.program_id(0)

    @pl.when(i == 0)
    def _():
        zero_scr[...] = jnp.zeros_like(zero_scr)
        for e in range(N_EXPERTS):
            @pl.when(offe_ref[e] > off_ref[e])
            def _():
                dst = xs_hbm.at[pl.ds(pl.multiple_of((offe_ref[e] - tile) * ROW_SUB, ROW_SUB), tile * ROW_SUB)]
                c = pltpu.make_async_copy(zero_scr, dst, zsem)
                c.start()
                c.wait()

        def fill_unused(j, c):
            dst = xs_hbm.at[pl.ds(pl.multiple_of(j * tile * ROW_SUB, ROW_SUB), tile * ROW_SUB)]
            cp = pltpu.make_async_copy(zero_scr, dst, zsem)
            cp.start()
            cp.wait()
            return c

        lax.fori_loop(offe_ref[N_EXPERTS - 1] // tile, xs_hbm.shape[0] // (tile * ROW_SUB), fill_unused, 0)

    def issue(r, c):
        for s in range(2):
            pltpu.make_async_copy(_row_tile(hn_ref, r), _row_tile(xs_hbm, dest_ref[0, s, r]), sem).start(priority=s)
        return c

    lax.fori_loop(0, tm, issue, 0, unroll=8)

    def drain(r, c):
        for s in range(2):
            pltpu.make_async_copy(_row_tile(hn_ref, 0), _row_tile(xs_hbm, 0), sem).wait()
        return c

    lax.fori_loop(0, tm, drain, 0, unroll=8)


def _dispatch(off, offe, dest3, hn3, n_rows, tile):
    nt, _, tm = dest3.shape
    return pl.pallas_call(
        functools.partial(_dispatch_kernel, tm=tm, tile=tile),
        grid_spec=pltpu.PrefetchScalarGridSpec(
            num_scalar_prefetch=2,
            grid=(nt,),
            in_specs=[pl.BlockSpec((1, 2, tm), lambda i, o, oe: (i, 0, 0), memory_space=pltpu.SMEM),
                      pl.BlockSpec((tm * ROW_SUB, LANES), lambda i, o, oe: (i, 0))],
            out_specs=pl.BlockSpec(memory_space=pl.ANY),
            scratch_shapes=[pltpu.VMEM((tile * ROW_SUB, LANES), F32), pltpu.SemaphoreType.DMA,
                            pltpu.SemaphoreType.DMA],
        ),
        out_shape=jax.ShapeDtypeStruct((n_rows * ROW_SUB, LANES), F32),
        compiler_params=_cparams(("arbitrary",)),
        name="moe_dispatch",
    )(off, offe, dest3, hn3)


def _experts_kernel(te_ref, nv_ref, xs_ref, wg_ref, wu_ref, wd_ref, ys_ref):
    @pl.when(pl.program_id(0) < nv_ref[0])
    def _():
        n_part = 2
        part = xs_ref.shape[0] // n_part
        rows = [pl.ds(k * part, part) for k in range(n_part)]
        x = [_tiles_to_rows(xs_ref.at[rows[k]]).astype(BF16) for k in range(n_part)]
        a = [_dot(x[k], wg_ref[0]) for k in range(n_part)]
        u = [_dot(x[k], wu_ref[0]) for k in range(n_part)]
        hid = [((a[k] * jax.nn.sigmoid(a[k])) * u[k]).astype(BF16) for k in range(n_part)]
        y = [_dot(hid[k], wd_ref[0]) for k in range(n_part)]
        for k in range(n_part):
            _rows_to_tiles(ys_ref.at[rows[k]], y[k])

    @pl.when(pl.program_id(0) >= nv_ref[0])
    def _():
        ys_ref[...] = jnp.zeros_like(ys_ref)


def _experts(te, nv, xs, wg, wu, wd, tile):
    E, D, H = wg.shape
    blk = (tile * ROW_SUB, LANES)

    def row_map(j, te, nv):
        return (j, 0)

    return pl.pallas_call(
        _experts_kernel,
        grid_spec=pltpu.PrefetchScalarGridSpec(
            num_scalar_prefetch=2,
            grid=(xs.shape[0] // blk[0],),
            in_specs=[pl.BlockSpec(blk, row_map),
                      pl.BlockSpec((1, D, H), lambda j, te, nv: (te[j], 0, 0)),
                      pl.BlockSpec((1, D, H), lambda j, te, nv: (te[j], 0, 0)),
                      pl.BlockSpec((1, H, D), lambda j, te, nv: (te[j], 0, 0))],
            out_specs=pl.BlockSpec(blk, row_map),
        ),
        out_shape=jax.ShapeDtypeStruct(xs.shape, F32),
        compiler_params=_cparams(("arbitrary",)),
        name="moe_experts",
    )(te, nv, xs, wg, wu, wd)


def _combine_kernel(dcur_ref, dnxt_ref, ys_hbm, h_ref, rt_ref, fin_ref, o_ref, buf, sem, *, tm):
    i = pl.program_id(0)
    n = pl.num_programs(0)
    slot = i % 2

    def issue(d_ref, sl):
        def body(r, c):
            for s in range(2):
                pltpu.make_async_copy(_row_tile(ys_hbm, d_ref[0, s, r]), _row_tile(buf.at[sl, s], r),
                                      sem.at[sl]).start(priority=s)
            return c
        lax.fori_loop(0, tm, body, 0, unroll=8)

    @pl.when(i == 0)
    def _():
        issue(dcur_ref, 0)

    @pl.when(i + 1 < n)
    def _():
        issue(dnxt_ref, 1 - slot)

    def drain(r, c):
        for s in range(2):
            pltpu.make_async_copy(_row_tile(ys_hbm, 0), _row_tile(buf.at[slot, s], 0), sem.at[slot]).wait()
        return c

    lax.fori_loop(0, tm, drain, 0, unroll=8)
    y = (h_ref[...] + rt_ref[:, 0:1] * _tiles_to_rows(buf.at[slot, 0])
         + rt_ref[:, 1:2] * _tiles_to_rows(buf.at[slot, 1]))
    ms = jnp.mean(y * y, axis=-1, keepdims=True)
    o_ref[...] = (y * lax.rsqrt(ms + NORM_EPS)) * fin_ref[...]


def _combine(dest3, ys, h, rt, final_norm):
    nt, _, tm = dest3.shape
    T, D = h.shape
    return pl.pallas_call(
        functools.partial(_combine_kernel, tm=tm),
        grid=(nt,),
        in_specs=[pl.BlockSpec((1, 2, tm), lambda i: (i, 0, 0), memory_space=pltpu.SMEM),
                  pl.BlockSpec((1, 2, tm), lambda i: (jnp.minimum(i + 1, nt - 1), 0, 0), memory_space=pltpu.SMEM),
                  pl.BlockSpec(memory_space=pl.ANY),
                  pl.BlockSpec((tm, D), lambda i: (i, 0)),
                  pl.BlockSpec((tm, LANES), lambda i: (i, 0)),
                  pl.BlockSpec((1, D), lambda i: (0, 0))],
        out_specs=pl.BlockSpec((tm, D), lambda i: (i, 0)),
        out_shape=jax.ShapeDtypeStruct((T, D), F32),
        scratch_shapes=[pltpu.VMEM((2, 2, tm * ROW_SUB, LANES), F32), pltpu.SemaphoreType.DMA((2,))],
        compiler_params=_cparams(("arbitrary",)),
        name="moe_combine",
    )(dest3, dest3, ys, h, rt, final_norm)


def _moe_kernel(hn_ref, h_ref, gates_ref, wg_ref, wu_ref, wd_ref, fin_ref, o_ref, acc_ref):
    e = pl.program_id(1)

    @pl.when(e == 0)
    def _():
        acc_ref[...] = h_ref[...]

    lane = lax.broadcasted_iota(jnp.int32, (1, LANES), 1)
    gcol = jnp.sum(jnp.where(lane == e, gates_ref[...], 0.0), axis=-1, keepdims=True)
    t = hn_ref[...]
    a = _dot(t, wg_ref[0].astype(BF16))
    u = _dot(t, wu_ref[0].astype(BF16))
    hid = (a * jax.nn.sigmoid(a)) * u * gcol
    acc_ref[...] += _dot(hid.astype(BF16), wd_ref[0].astype(BF16))

    @pl.when(e == pl.num_programs(1) - 1)
    def _():
        y = acc_ref[...]
        ms = jnp.mean(y * y, axis=-1, keepdims=True)
        o_ref[...] = (y * lax.rsqrt(ms + NORM_EPS)) * fin_ref[...]


def _moe(hn, h, gates, w_gate, w_up, w_down, final_norm, tm):
    T, D = h.shape
    E, _, H = w_gate.shape
    return pl.pallas_call(
        _moe_kernel,
        grid=(T // tm, E),
        in_specs=[
            pl.BlockSpec((tm, D), lambda i, e: (i, 0)),
            pl.BlockSpec((tm, D), lambda i, e: (i, 0)),
            pl.BlockSpec((tm, LANES), lambda i, e: (i, 0)),
            pl.BlockSpec((1, D, H), lambda i, e: (e, 0, 0)),
            pl.BlockSpec((1, D, H), lambda i, e: (e, 0, 0)),
            pl.BlockSpec((1, H, D), lambda i, e: (e, 0, 0)),
            pl.BlockSpec((1, D), lambda i, e: (0, 0)),
        ],
        out_specs=pl.BlockSpec((tm, D), lambda i, e: (i, 0)),
        out_shape=jax.ShapeDtypeStruct((T, D), F32),
        scratch_shapes=[pltpu.VMEM((tm, D), F32)],
        compiler_params=_cparams(("arbitrary", "arbitrary")),
        name="moe",
    )(hn, h, gates, w_gate, w_up, w_down, final_norm)


def kernel(x, attn_norm, w_in, hg_lb_logits, hg_norm, fox_f_bias, fox_norm, w_out, ffn_norm,
           w_group, b_group, w_expert, b_expert, w_gate, w_up, w_down, final_norm):
    B, S, D = x.shape
    T = B * S
    assert w_in.shape[0] == 1, "single-layer block"
    hw = HG_HEADS * HG_DK
    fw = FOX_HEADS * FOX_DH
    wi = w_in[0]
    o = [0, hw, 2 * hw, 3 * hw, 4 * hw, 4 * hw + fw, 4 * hw + 2 * fw, 4 * hw + 3 * fw]
    w_all = jnp.concatenate(
        [wi[:, o[0]:o[1]], wi[:, o[2]:o[3]], wi[:, o[3]:o[4]], wi[:, o[4]:o[7]], wi[:, o[1]:o[2]],
         jnp.pad(wi[:, o[7]:], ((0, 0), (0, LANES - FOX_HEADS)))], axis=1).astype(BF16)

    x2 = x.reshape(T, D)
    tm_in = min(512, S)
    u, hgf, foxf = _inproj(x2, attn_norm.reshape(1, D), w_all, B, S, tm_in)

    cum = _foxcum(foxf.reshape(B * FOX_HEADS, S), jnp.tile(fox_f_bias[0], B).reshape(B * FOX_HEADS, 1),
                  min(256, S))
    tq = min(256, S)
    u3 = u.reshape(B, S, -1)
    o_b = _foxt(u3, cum.reshape(B, FOX_HEADS // 2, 2, S // tq, tq),
               jnp.tile(fox_norm[0], 2).reshape(1, LANES), tq)
    o_a = _hgrn2w(u3, hgf.reshape(B, S, hw), hg_lb_logits, jnp.tile(hg_norm[0], HG_HEADS).reshape(1, hw))

    wr = jnp.pad(jnp.concatenate([w_group[0], w_expert[0]], axis=1),
                 ((0, 0), (0, LANES - N_GROUPS - N_EXPERTS)))
    br = jnp.pad(jnp.concatenate([b_group[0], b_expert[0]]), (0, LANES - N_GROUPS - N_EXPERTS)).reshape(1, LANES)
    h, hn3, rt, r8 = _outproj(o_a.reshape(T, hw), o_b.reshape(T, fw), x2, w_out[0].astype(BF16),
                              ffn_norm[0].reshape(1, D), wr, br, min(512, T))

    blk = min(1024, T)
    n_rows = 2 * T + N_EXPERTS * MOE_TILE
    n_tiles = n_rows // MOE_TILE
    rank, cnt = _rank(r8, blk)
    dest, meta, te8 = _dest(r8, rank, cnt, blk, MOE_TILE, -(-n_tiles // LANES) * LANES)
    off, offe = meta[:, 0], meta[:, 1]
    te, nv = te8[0, :n_tiles], te8[1, :1]

    def tiles_of(tm):
        return dest[0:2].reshape(2, T // tm, tm).transpose(1, 0, 2)

    xs = _dispatch(off, offe, tiles_of(min(DISP_TM, T)), hn3, n_rows, MOE_TILE)
    ys = _experts(te, nv, xs, w_gate[0].astype(BF16), w_up[0].astype(BF16), w_down[0].astype(BF16), MOE_TILE)
    out = _combine(tiles_of(min(COMB_TM, T)), ys, h, rt, final_norm.reshape(1, D))
    return out.reshape(B, S, D)
```

```python
import functools

import jax
import jax.numpy as jnp
from jax import lax
from jax.experimental import pallas as pl
from jax.experimental.pallas import tpu as pltpu

F32 = jnp.float32
BF16 = jnp.bfloat16

NORM_EPS = 1e-6
NEG_INF = -1e30

LANES = 128
HG_HEADS = 4
HG_DK = 128
FOX_HEADS = 8
FOX_DH = 64
N_GROUPS = 4
PER_GROUP = 4
N_EXPERTS = 16

SUB = 16
SUPER = 128
HG_STEP = 2
MOE_TILE = 512
DISP_TM = 1024
COMB_TM = 512
VMEM_LIMIT = 56 * 1024 * 1024


def _cparams(sem):
    return pltpu.CompilerParams(dimension_semantics=sem, vmem_limit_bytes=VMEM_LIMIT)


def _split3(x):
    hi = x.astype(BF16)
    r1 = x - hi.astype(F32)
    mid = r1.astype(BF16)
    lo = (r1 - mid.astype(F32)).astype(BF16)
    return hi, mid, lo


def _dot(a, b):
    return jnp.dot(a, b, preferred_element_type=F32)


def _dot_nt(a, b):
    return lax.dot_general(a, b, (((1,), (1,)), ((), ())), preferred_element_type=F32)


def _dot3(m_bf16, x_f32):
    hi, mid, lo = _split3(x_f32)
    return _dot(m_bf16, hi) + _dot(m_bf16, mid) + _dot(m_bf16, lo)


def _inproj_kernel(x_ref, g_ref, w_ref, u_ref, hgf_ref, foxf_ref, xn_ref, *, n_main, col_chunk):
    x = x_ref[...]
    ms = jnp.mean(x * x, axis=-1, keepdims=True)
    xn_ref[...] = ((x * lax.rsqrt(ms + NORM_EPS)) * g_ref[...]).astype(BF16)
    for j in range(n_main // col_chunk):
        r = _dot(xn_ref[...], w_ref[:, j * col_chunk:(j + 1) * col_chunk])
        u_ref[:, j * col_chunk:(j + 1) * col_chunk] = r.astype(BF16)
    r = _dot(xn_ref[...], w_ref[:, n_main:n_main + 4 * LANES])
    hgf_ref[...] = r
    r = _dot(xn_ref[...], w_ref[:, n_main + 4 * LANES:n_main + 5 * LANES])
    foxf_ref[0] = r.T[0:FOX_HEADS, :]


def _inproj(x2, attn_norm, w_all, B, S, tm):
    T, D = x2.shape
    n_main = 6 * 512
    per_seq = S // tm
    return pl.pallas_call(
        functools.partial(_inproj_kernel, n_main=n_main, col_chunk=512),
        grid=(T // tm,),
        in_specs=[
            pl.BlockSpec((tm, D), lambda i: (i, 0)),
            pl.BlockSpec((1, D), lambda i: (0, 0)),
            pl.BlockSpec(w_all.shape, lambda i: (0, 0)),
        ],
        out_specs=[
            pl.BlockSpec((tm, n_main), lambda i: (i, 0)),
            pl.BlockSpec((tm, 4 * LANES), lambda i: (i, 0)),
            pl.BlockSpec((1, FOX_HEADS, tm), lambda i: (i // per_seq, 0, i % per_seq)),
        ],
        out_shape=[
            jax.ShapeDtypeStruct((T, n_main), BF16),
            jax.ShapeDtypeStruct((T, 4 * LANES), F32),
            jax.ShapeDtypeStruct((B, FOX_HEADS, S), F32),
        ],
        scratch_shapes=[pltpu.VMEM((tm, D), BF16)],
        compiler_params=_cparams(("arbitrary",)),
        name="inproj",
    )(x2, attn_norm, w_all)


def _foxcum_kernel(f_ref, bias_ref, cum_ref, off_ref):
    blk = f_ref.shape[1]

    @pl.when(pl.program_id(0) == 0)
    def _():
        off_ref[...] = jnp.zeros_like(off_ref)

    r_i = lax.broadcasted_iota(jnp.int32, (blk, blk), 0)
    c_i = lax.broadcasted_iota(jnp.int32, (blk, blk), 1)
    upper = (r_i <= c_i).astype(BF16)
    z = f_ref[...] + bias_ref[...]
    logf = jnp.minimum(z, 0.0) - jnp.log(1.0 + jnp.exp(-jnp.abs(z)))
    hi, mid, lo = _split3(logf)
    cum = _dot(hi, upper) + _dot(mid, upper) + _dot(lo, upper) + off_ref[...]
    cum_ref[...] = cum
    off_ref[...] = cum[:, blk - 1:blk]


def _foxcum(foxf2, bias_col, blk):
    R, S = foxf2.shape
    return pl.pallas_call(
        _foxcum_kernel,
        grid=(S // blk,),
        in_specs=[pl.BlockSpec((R, blk), lambda j: (0, j)), pl.BlockSpec((R, 1), lambda j: (0, 0))],
        out_specs=pl.BlockSpec((R, blk), lambda j: (0, j)),
        out_shape=jax.ShapeDtypeStruct((R, S), F32),
        scratch_shapes=[pltpu.VMEM((R, 1), F32)],
        compiler_params=_cparams(("arbitrary",)),
        name="foxcum",
    )(foxf2, bias_col)


def _foxt_kernel(q_ref, k_ref, v_ref, cum_ref, gain_ref, o_ref,
                 kaug_scr, vt_scr, qa_scr, s_scr, p_scr, m_scr, al_scr, acc_scr, *, tq):
    qi = pl.program_id(1)
    nkb = vt_scr.shape[1]
    lane = lax.broadcasted_iota(jnp.int32, (1, LANES), 1)
    rid = lax.broadcasted_iota(jnp.int32, (LANES, 1), 0)
    lo_lane, lo_row = lane < FOX_DH, rid < FOX_DH
    aug_lane = [(lane >= FOX_DH) & (lane < FOX_DH + 3), lane < 3]

    @pl.when(qi == 0)
    def _():
        for kb in range(nkb):
            rows = slice(kb * tq, (kb + 1) * tq)
            for p in range(FOX_HEADS // 2):
                ls = slice(p * LANES, (p + 1) * LANES)
                pe = _split3(cum_ref[0, p, 0, kb:kb + 1, :])
                po = _split3(cum_ref[0, p, 1, kb:kb + 1, :])
                x = jnp.zeros((LANES, tq), F32)
                for j in range(3):
                    x = jnp.where(rid == FOX_DH + j, pe[j].astype(F32), x)
                    x = jnp.where(rid == j, po[j].astype(F32), x)
                aug = x.T
                kf = k_ref[0, rows, ls].astype(F32)
                kaug_scr[2 * p, rows, :] = jnp.where(lo_lane, kf, aug).astype(BF16)
                kaug_scr[2 * p + 1, rows, :] = jnp.where(lo_lane, aug, kf).astype(BF16)
                vt = v_ref[0, rows, ls].astype(F32).T
                vt_scr[2 * p, kb] = jnp.where(lo_row, vt, 1.0).astype(BF16)
                vt_scr[2 * p + 1, kb] = jnp.where(lo_row, 1.0, vt).astype(BF16)

    for p in range(FOX_HEADS // 2):
        q = q_ref[0, :, p * LANES:(p + 1) * LANES].astype(F32) * (FOX_DH ** -0.5)
        qa_scr[2 * p] = jnp.where(lo_lane, q, jnp.where(aug_lane[0], -1.0, 0.0)).astype(BF16)
        qa_scr[2 * p + 1] = jnp.where(lo_lane, jnp.where(aug_lane[1], -1.0, 0.0), q).astype(BF16)
    m_scr[...] = jnp.full(m_scr.shape, NEG_INF, F32)
    acc_scr[...] = jnp.zeros_like(acc_scr)
    krow = lax.broadcasted_iota(jnp.int32, (tq, LANES), 0)
    qcol = lax.broadcasted_iota(jnp.int32, (tq, LANES), 1)

    def step(kb, masked):
        r0 = pl.multiple_of(kb * tq, tq)
        for hd in range(FOX_HEADS):
            s_scr[hd] = _dot_nt(kaug_scr[hd, pl.ds(r0, tq), :], qa_scr[hd])
        for hd in range(FOX_HEADS):
            for lt in range(tq // LANES):
                ls = slice(lt * LANES, (lt + 1) * LANES)
                s = s_scr[hd, :, ls]
                if masked:
                    s = jnp.where(krow <= qcol + lt * LANES, s, NEG_INF)
                m = m_scr[hd:hd + 1, ls]
                m_new = jnp.maximum(m, jnp.max(s, axis=0, keepdims=True))
                p_scr[hd, :, ls] = jnp.exp(s - m_new).astype(BF16)
                al_scr[hd:hd + 1, ls] = jnp.exp(m - m_new)
                m_scr[hd:hd + 1, ls] = m_new
        for hd in range(FOX_HEADS):
            acc_scr[hd] = acc_scr[hd] * al_scr[hd:hd + 1, :] + _dot(vt_scr[hd, kb], p_scr[hd])

    def body(kb, c):
        step(kb, False)
        return c

    lax.fori_loop(0, qi, body, 0)
    step(qi, True)
    for p in range(FOX_HEADS // 2):
        ae, ao = acc_scr[2 * p], acc_scr[2 * p + 1]
        o = jnp.where(lo_row, ae / ae[FOX_DH:FOX_DH + 1, :], ao / ao[0:1, :])
        o2 = o * o
        ms_e = jnp.sum(jnp.where(lo_row, o2, 0.0), axis=0, keepdims=True) * (1.0 / FOX_DH)
        ms_o = jnp.sum(jnp.where(lo_row, 0.0, o2), axis=0, keepdims=True) * (1.0 / FOX_DH)
        o = o * lax.rsqrt(jnp.where(lo_row, ms_e, ms_o) + NORM_EPS)
        o_ref[0, :, p * LANES:(p + 1) * LANES] = (o.T * gain_ref[...]).astype(BF16)


def _foxt(u3, cum5, gain2, tq):
    B, S, _ = u3.shape
    nq = S // tq
    fw = FOX_HEADS * FOX_DH
    qc, kc, vc = 3, 4, 5
    return pl.pallas_call(
        functools.partial(_foxt_kernel, tq=tq),
        grid=(B, nq),
        in_specs=[
            pl.BlockSpec((1, tq, fw), lambda b, i: (b, i, qc)),
            pl.BlockSpec((1, S, fw), lambda b, i: (b, 0, kc)),
            pl.BlockSpec((1, S, fw), lambda b, i: (b, 0, vc)),
            pl.BlockSpec((1, FOX_HEADS // 2, 2, nq, tq), lambda b, i: (b, 0, 0, 0, 0)),
            pl.BlockSpec((1, LANES), lambda b, i: (0, 0)),
        ],
        out_specs=pl.BlockSpec((1, tq, fw), lambda b, i: (b, i, 0)),
        out_shape=jax.ShapeDtypeStruct((B, S, fw), BF16),
        scratch_shapes=[
            pltpu.VMEM((FOX_HEADS, S, LANES), BF16),
            pltpu.VMEM((FOX_HEADS, nq, LANES, tq), BF16),
            pltpu.VMEM((FOX_HEADS, tq, LANES), BF16),
            pltpu.VMEM((FOX_HEADS, tq, tq), F32),
            pltpu.VMEM((FOX_HEADS, tq, tq), BF16),
            pltpu.VMEM((FOX_HEADS, tq), F32),
            pltpu.VMEM((FOX_HEADS, tq), F32),
            pltpu.VMEM((FOX_HEADS, LANES, tq), F32),
        ],
        compiler_params=_cparams(("arbitrary", "arbitrary")),
        name="fox",
    )(u3, u3, u3, cum5, gain2)


def _hgrn2w_kernel(q_ref, i_ref, g_ref, f_ref, lbl_ref, gain_ref, o_ref, b_scr, q_scr, c_scr, st_scr):
    S, W = q_ref.shape[1], q_ref.shape[2]
    nh = W // LANES
    rows_step = HG_STEP * SUPER
    n_sub = SUPER // SUB
    hsl = [slice(h * LANES, (h + 1) * LANES) for h in range(nh)]
    a = lbl_ref[...]
    am = jnp.max(a, axis=0, keepdims=True)
    ea = jnp.exp(a - am)
    lb = ea[0:1, :] / (ea[0:1, :] + ea[1:2, :])

    r_i = lax.broadcasted_iota(jnp.int32, (SUPER, SUPER), 0)
    c_i = lax.broadcasted_iota(jnp.int32, (SUPER, SUPER), 1)
    same = (r_i // SUB) == (c_i // SUB)
    tri = (same & (c_i <= r_i)).astype(BF16)
    blk1 = same.astype(BF16)
    ones = jnp.ones((LANES, LANES), BF16)
    lane = lax.broadcasted_iota(jnp.int32, (1, LANES), 1)
    trow = lax.broadcasted_iota(jnp.int32, (8, W), 0)

    st_scr[...] = jnp.zeros_like(st_scr)

    def superblock(sb, _):
        r0 = pl.multiple_of(sb * rows_step, rows_step)
        hq = 0.5 * q_ref[0, pl.ds(r0, rows_step), :].astype(F32)
        q = hq + hq * jnp.tanh(hq)
        f = lb + (1.0 - lb) * (0.5 + 0.5 * jnp.tanh(0.5 * f_ref[0, pl.ds(r0, rows_step), :]))
        lf2 = jnp.log2(f)
        halves_r = [slice(k * SUPER, (k + 1) * SUPER) for k in range(HG_STEP)]
        b2 = jnp.concatenate([_dot3(tri, lf2[hr]) for hr in halves_r], axis=0)
        bt2 = jnp.concatenate([_dot3(blk1, lf2[hr]) for hr in halves_r], axis=0)
        c2 = b2 - jnp.log2(1.0 - f)
        qt = (q * jnp.exp2(b2)).astype(BF16)
        kh = jnp.exp2(bt2 - c2).astype(BF16)
        gam = jnp.exp2(bt2)
        b_scr[...] = b2
        q_scr[...] = q
        c_scr[...] = c2
        v = i_ref[0, pl.ds(r0, rows_step), :]

        st = [st_scr[h] for h in range(nh)]
        o_inter = [[] for _ in range(nh)]
        for c in range(HG_STEP * n_sub):
            rows = slice(c * SUB, (c + 1) * SUB)
            for h in range(nh):
                o_inter[h].append(_dot_nt(qt[rows, hsl[h]], st[h].astype(BF16)))
                dst = lax.dot_general(v[rows, hsl[h]], kh[rows, hsl[h]], (((0,), (0,)), ((), ())),
                                      preferred_element_type=F32)
                st[h] = gam[c * SUB:c * SUB + 1, hsl[h]] * st[h] + dst
        for h in range(nh):
            st_scr[h] = st[h]

        a_rows = [[] for _ in range(nh)]
        for c in range(HG_STEP * n_sub):
            base = c * SUB
            col0 = base % SUPER
            tiles = []
            lo_half = []
            for j in range(SUB):
                cs = c_scr[base + j:base + j + 1, :]
                halves = []
                for hf in range(2):
                    if hf == 0 and j >= 8:
                        continue
                    t0 = base + 8 * hf
                    p = q_scr[t0:t0 + 8, :] * jnp.exp2(b_scr[t0:t0 + 8, :] - cs)
                    if j // 8 == hf:
                        p = jnp.where(trow >= (j % 8), p, 0.0)
                    halves.append(p)
                if j < 8:
                    tiles.append(jnp.concatenate(halves, axis=0))
                else:
                    lo_half.append(halves[0])
            for m in range(4):
                tiles.append(jnp.concatenate([lo_half[2 * m], lo_half[2 * m + 1]], axis=0))
            pc = jnp.concatenate(tiles, axis=0).astype(BF16)
            for h in range(nh):
                rc = _dot(pc[:, hsl[h]], ones)
                a_top = jnp.zeros((8, LANES), F32)
                a_bot = jnp.zeros((8, LANES), F32)
                for j in range(8):
                    sel = lane == (col0 + j)
                    a_top = jnp.where(sel, rc[16 * j:16 * j + 8, :], a_top)
                    a_bot = jnp.where(sel, rc[16 * j + 8:16 * j + 16, :], a_bot)
                for j in range(8, SUB):
                    sel = lane == (col0 + j)
                    a_bot = jnp.where(sel, rc[128 + 8 * (j - 8):128 + 8 * (j - 8) + 8, :], a_bot)
                a_rows[h] += [a_top, a_bot]

        outs = []
        for h in range(nh):
            parts = []
            for k, hr in enumerate(halves_r):
                tl = a_rows[h][k * 2 * n_sub:(k + 1) * 2 * n_sub]
                a_blk = jnp.concatenate(tl, axis=0).astype(BF16)
                parts.append(_dot(a_blk, v[hr, hsl[h]]))
            o = jnp.concatenate(o_inter[h], axis=0) + jnp.concatenate(parts, axis=0)
            ms = jnp.mean(o * o, axis=-1, keepdims=True)
            outs.append(o * lax.rsqrt(ms + NORM_EPS))
        o = jnp.concatenate(outs, axis=1) * gain_ref[...]
        hg = 0.5 * g_ref[0, pl.ds(r0, rows_step), :].astype(F32)
        o_ref[0, pl.ds(r0, rows_step), :] = (o * (hg + hg * jnp.tanh(hg))).astype(BF16)
        return 0

    lax.fori_loop(0, S // rows_step, superblock, 0)


def _hgrn2w(u3, hgf3, lb_logits, gain_w):
    B, S, _ = u3.shape
    W = HG_HEADS * HG_DK
    return pl.pallas_call(
        _hgrn2w_kernel,
        grid=(B,),
        in_specs=[
            pl.BlockSpec((1, S, W), lambda b: (b, 0, 0)),
            pl.BlockSpec((1, S, W), lambda b: (b, 0, 1)),
            pl.BlockSpec((1, S, W), lambda b: (b, 0, 2)),
            pl.BlockSpec((1, S, W), lambda b: (b, 0, 0)),
            pl.BlockSpec((2, W), lambda b: (0, 0)),
            pl.BlockSpec((1, W), lambda b: (0, 0)),
        ],
        out_specs=pl.BlockSpec((1, S, W), lambda b: (b, 0, 0)),
        out_shape=jax.ShapeDtypeStruct((B, S, W), BF16),
        scratch_shapes=[pltpu.VMEM((HG_STEP * SUPER, W), F32)] * 3 + [pltpu.VMEM((HG_HEADS, LANES, LANES), F32)],
        compiler_params=_cparams(("arbitrary",)),
        name="hgrn2",
    )(u3, u3, u3, hgf3, lb_logits, gain_w)


ROW_SUB = 8


def _rows_to_tiles(ref, x):
    m = x.shape[0]
    for c in range(ROW_SUB):
        ref[pl.ds(c, m, stride=ROW_SUB), :] = x[:, c * LANES:(c + 1) * LANES]


def _tiles_to_rows(ref):
    m = ref.shape[0] // ROW_SUB
    return jnp.concatenate([ref[pl.ds(c, m, stride=ROW_SUB), :] for c in range(ROW_SUB)], axis=1)


def _row_tile(ref, r):
    return ref.at[pl.ds(pl.multiple_of(r * ROW_SUB, ROW_SUB), ROW_SUB)]


def _outproj_kernel(oa_ref, ob_ref, x_ref, wo_ref, g_ref, wr_ref, br_ref, h_ref, hn_ref, rt_ref, r8_ref):
    half = oa_ref.shape[1]
    h = x_ref[...] + _dot(oa_ref[...], wo_ref[0:half, :]) + _dot(ob_ref[...], wo_ref[half:2 * half, :])
    h_ref[...] = h
    ms = jnp.mean(h * h, axis=-1, keepdims=True)
    hn = (h * lax.rsqrt(ms + NORM_EPS)) * g_ref[...]
    _rows_to_tiles(hn_ref, hn)
    hn_hi = hn.astype(BF16)
    hn_lo = (hn - hn_hi.astype(F32)).astype(BF16)
    wr = wr_ref[...]
    wr_hi = wr.astype(BF16)
    wr_lo = (wr - wr_hi.astype(F32)).astype(BF16)
    logits = _dot(hn_hi, wr_hi) + _dot(hn_lo, wr_hi) + _dot(hn_hi, wr_lo) + br_ref[...]
    lt = logits.T
    g = [lt[i:i + 1, :] for i in range(N_GROUPS)]
    gm = jnp.maximum(jnp.maximum(g[0], g[1]), jnp.maximum(g[2], g[3]))
    gsel = jnp.where(g[0] == gm, 0, jnp.where(g[1] == gm, 1, jnp.where(g[2] == gm, 2, 3)))
    pg = 1.0 / (jnp.exp(g[0] - gm) + jnp.exp(g[1] - gm) + jnp.exp(g[2] - gm) + jnp.exp(g[3] - gm))
    e = []
    for i in range(PER_GROUP):
        rows = [lt[N_GROUPS + PER_GROUP * gg + i:N_GROUPS + PER_GROUP * gg + i + 1, :] for gg in range(N_GROUPS)]
        e.append(jnp.where(gsel == 0, rows[0], jnp.where(gsel == 1, rows[1], jnp.where(gsel == 2, rows[2], rows[3]))))
    e1 = jnp.maximum(jnp.maximum(e[0], e[1]), jnp.maximum(e[2], e[3]))
    i1 = jnp.where(e[0] == e1, 0, jnp.where(e[1] == e1, 1, jnp.where(e[2] == e1, 2, 3)))
    ex = [jnp.where(i1 == i, -jnp.inf, e[i]) for i in range(PER_GROUP)]
    e2 = jnp.maximum(jnp.maximum(ex[0], ex[1]), jnp.maximum(ex[2], ex[3]))
    i2 = jnp.where(ex[0] == e2, 0, jnp.where(ex[1] == e2, 1, jnp.where(ex[2] == e2, 2, 3)))
    r = jnp.exp(e2 - e1)
    w1 = pg / (1.0 + r)
    w2 = w1 * r
    x1 = gsel * PER_GROUP + i1
    x2 = gsel * PER_GROUP + i2
    ridx = lax.broadcasted_iota(jnp.int32, (LANES, 1), 0)
    wt = jnp.where(ridx == 0, w1, 0.0) + jnp.where(ridx == 1, w2, 0.0)
    rt_ref[...] = wt.T
    sidx = lax.broadcasted_iota(jnp.int32, (8, 1), 0)
    r8_ref[...] = (jnp.where(sidx == 0, x1, 0) + jnp.where(sidx == 1, x2, 0)).astype(F32)


def _outproj(oa, ob, x2, w_out, ffn_norm, wr, br, tm):
    T, D = x2.shape
    half = oa.shape[1]
    return pl.pallas_call(
        _outproj_kernel,
        grid=(T // tm,),
        in_specs=[
            pl.BlockSpec((tm, half), lambda i: (i, 0)),
            pl.BlockSpec((tm, half), lambda i: (i, 0)),
            pl.BlockSpec((tm, D), lambda i: (i, 0)),
            pl.BlockSpec(w_out.shape, lambda i: (0, 0)),
            pl.BlockSpec((1, D), lambda i: (0, 0)),
            pl.BlockSpec(wr.shape, lambda i: (0, 0)),
            pl.BlockSpec((1, LANES), lambda i: (0, 0)),
        ],
        out_specs=[
            pl.BlockSpec((tm, D), lambda i: (i, 0)),
            pl.BlockSpec((tm * ROW_SUB, LANES), lambda i: (i, 0)),
            pl.BlockSpec((tm, LANES), lambda i: (i, 0)),
            pl.BlockSpec((8, tm), lambda i: (0, i)),
        ],
        out_shape=[
            jax.ShapeDtypeStruct((T, D), F32),
            jax.ShapeDtypeStruct((T * ROW_SUB, LANES), F32),
            jax.ShapeDtypeStruct((T, LANES), F32),
            jax.ShapeDtypeStruct((8, T), F32),
        ],
        compiler_params=_cparams(("arbitrary",)),
        name="outproj",
    )(oa, ob, x2, w_out, ffn_norm, wr, br)


def _rank_kernel(r8_ref, rank_ref, cnt_ref, off_scr):
    blk = r8_ref.shape[1]

    @pl.when(pl.program_id(0) == 0)
    def _():
        off_scr[...] = jnp.zeros_like(off_scr)

    x1, x2 = r8_ref[0:1, :], r8_ref[1:2, :]
    eid = lax.broadcasted_iota(jnp.int32, (N_EXPERTS, 1), 0).astype(F32)
    r_i = lax.broadcasted_iota(jnp.int32, (blk, blk), 0)
    c_i = lax.broadcasted_iota(jnp.int32, (blk, blk), 1)
    upper = (r_i <= c_i).astype(BF16)
    hit = ((x1 == eid) | (x2 == eid)).astype(BF16)
    cum = _dot(hit, upper) + off_scr[...]
    rank1 = jnp.sum(jnp.where(x1 == eid, cum, 0.0), axis=0, keepdims=True) - 1.0
    rank2 = jnp.sum(jnp.where(x2 == eid, cum, 0.0), axis=0, keepdims=True) - 1.0
    sidx = lax.broadcasted_iota(jnp.int32, (8, 1), 0)
    rank_ref[...] = jnp.where(sidx == 0, rank1, 0.0) + jnp.where(sidx == 1, rank2, 0.0)
    off_scr[...] = cum[:, blk - 1:blk]
    cnt_ref[...] = jnp.broadcast_to(cum[:, blk - 1:blk], cnt_ref.shape)


def _rank(r8, blk):
    T = r8.shape[1]
    return pl.pallas_call(
        _rank_kernel,
        grid=(T // blk,),
        in_specs=[pl.BlockSpec((8, blk), lambda j: (0, j))],
        out_specs=[pl.BlockSpec((8, blk), lambda j: (0, j)), pl.BlockSpec((N_EXPERTS, LANES), lambda j: (0, 0))],
        out_shape=[jax.ShapeDtypeStruct((8, T), F32), jax.ShapeDtypeStruct((N_EXPERTS, LANES), F32)],
        scratch_shapes=[pltpu.VMEM((N_EXPERTS, 1), F32)],
        compiler_params=_cparams(("arbitrary",)),
        name="moe_rank",
    )(r8)


def _dest_kernel(r8_ref, rank_ref, cnt_ref, dest_ref, meta_ref, te_ref, *, tile):
    cnt = cnt_ref[...]
    cntp = jnp.floor((cnt + (tile - 1.0)) * (1.0 / tile)) * tile
    r_i = lax.broadcasted_iota(jnp.int32, (N_EXPERTS, N_EXPERTS), 0)
    c_i = lax.broadcasted_iota(jnp.int32, (N_EXPERTS, N_EXPERTS), 1)
    off = _dot3((c_i < r_i).astype(BF16), cntp)
    offe = off + cntp
    x1, x2 = r8_ref[0:1, :], r8_ref[1:2, :]
    eid = lax.broadcasted_iota(jnp.int32, (N_EXPERTS, 1), 0).astype(F32)
    off_c = off[:, 0:1]
    d1 = jnp.sum(jnp.where(x1 == eid, off_c, 0.0), axis=0, keepdims=True) + rank_ref[0:1, :]
    d2 = jnp.sum(jnp.where(x2 == eid, off_c, 0.0), axis=0, keepdims=True) + rank_ref[1:2, :]
    sidx = lax.broadcasted_iota(jnp.int32, (8, 1), 0)
    dest_ref[...] = (jnp.where(sidx == 0, d1, 0.0) + jnp.where(sidx == 1, d2, 0.0)).astype(jnp.int32)
    lane = lax.broadcasted_iota(jnp.int32, (1, LANES), 1)
    meta_ref[...] = jnp.where(lane == 0, off, offe).astype(jnp.int32)
    start = lax.broadcasted_iota(jnp.int32, (1, te_ref.shape[1]), 1).astype(F32) * tile
    te = jnp.sum((start >= offe[:, 0:1]).astype(F32), axis=0, keepdims=True)
    nvalid = offe[N_EXPERTS - 1:N_EXPERTS, 0:1] * (1.0 / tile)
    te_ref[...] = (jnp.where(sidx == 0, jnp.minimum(te, N_EXPERTS - 1.0), 0.0)
                   + jnp.where(sidx == 1, nvalid, 0.0)).astype(jnp.int32)


def _dest(r8, rank, cnt, blk, tile, n_tiles_pad):
    T = r8.shape[1]
    return pl.pallas_call(
        functools.partial(_dest_kernel, tile=float(tile)),
        grid=(T // blk,),
        in_specs=[pl.BlockSpec((8, blk), lambda j: (0, j)), pl.BlockSpec((8, blk), lambda j: (0, j)),
                  pl.BlockSpec((N_EXPERTS, LANES), lambda j: (0, 0))],
        out_specs=[pl.BlockSpec((8, blk), lambda j: (0, j)), pl.BlockSpec((N_EXPERTS, LANES), lambda j: (0, 0)),
                   pl.BlockSpec((8, n_tiles_pad), lambda j: (0, 0))],
        out_shape=[jax.ShapeDtypeStruct((8, T), jnp.int32), jax.ShapeDtypeStruct((N_EXPERTS, LANES), jnp.int32),
                   jax.ShapeDtypeStruct((8, n_tiles_pad), jnp.int32)],
        compiler_params=_cparams(("arbitrary",)),
        name="moe_dest",
    )(r8, rank, cnt)


def _dispatch_kernel(off_ref, offe_ref, dest_ref, hn_ref, xs_hbm, zero_scr, sem, zsem, *, tm, tile):
    i = pl.program_id(0)

    @pl.when(i == 0)
    def _():
        zero_scr[...] = jnp.zeros_like(zero_scr)
        for e in range(N_EXPERTS):
            @pl.when(offe_ref[e] > off_ref[e])
            def _():
                dst = xs_hbm.at[pl.ds(pl.multiple_of((offe_ref[e] - tile) * ROW_SUB, ROW_SUB), tile * ROW_SUB)]
                c = pltpu.make_async_copy(zero_scr, dst, zsem)
                c.start()
                c.wait()

        def fill_unused(j, c):
            dst = xs_hbm.at[pl.ds(pl.multiple_of(j * tile * ROW_SUB, ROW_SUB), tile * ROW_SUB)]
            cp = pltpu.make_async_copy(zero_scr, dst, zsem)
            cp.start()
            cp.wait()
            return c

        lax.fori_loop(offe_ref[N_EXPERTS - 1] // tile, xs_hbm.shape[0] // (tile * ROW_SUB), fill_unused, 0)

    def issue(r, c):
        for s in range(2):
            pltpu.make_async_copy(_row_tile(hn_ref, r), _row_tile(xs_hbm, dest_ref[0, s, r]), sem).start(priority=s)
        return c

    lax.fori_loop(0, tm, issue, 0, unroll=8)

    def drain(r, c):
        for s in range(2):
            pltpu.make_async_copy(_row_tile(hn_ref, 0), _row_tile(xs_hbm, 0), sem).wait()
        return c

    lax.fori_loop(0, tm, drain, 0, unroll=8)


def _dispatch(off, offe, dest3, hn3, n_rows, tile):
    nt, _, tm = dest3.shape
    return pl.pallas_call(
        functools.partial(_dispatch_kernel, tm=tm, tile=tile),
        grid_spec=pltpu.PrefetchScalarGridSpec(
            num_scalar_prefetch=2,
            grid=(nt,),
            in_specs=[pl.BlockSpec((1, 2, tm), lambda i, o, oe: (i, 0, 0), memory_space=pltpu.SMEM),
                      pl.BlockSpec((tm * ROW_SUB, LANES), lambda i, o, oe: (i, 0))],
            out_specs=pl.BlockSpec(memory_space=pl.ANY),
            scratch_shapes=[pltpu.VMEM((tile * ROW_SUB, LANES), F32), pltpu.SemaphoreType.DMA,
                            pltpu.SemaphoreType.DMA],
        ),
        out_shape=jax.ShapeDtypeStruct((n_rows * ROW_SUB, LANES), F32),
        compiler_params=_cparams(("arbitrary",)),
        name="moe_dispatch",
    )(off, offe, dest3, hn3)


def _experts_kernel(te_ref, nv_ref, xs_ref, wg_ref, wu_ref, wd_ref, ys_ref):
    @pl.when(pl.program_id(0) < nv_ref[0])
    def _():
        n_part = 2
        part = xs_ref.shape[0] // n_part
        rows = [pl.ds(k * part, part) for k in range(n_part)]
        x = [_tiles_to_rows(xs_ref.at[rows[k]]).astype(BF16) for k in range(n_part)]
        a = [_dot(x[k], wg_ref[0]) for k in range(n_part)]
        u = [_dot(x[k], wu_ref[0]) for k in range(n_part)]
        hid = [((a[k] * jax.nn.sigmoid(a[k])) * u[k]).astype(BF16) for k in range(n_part)]
        y = [_dot(hid[k], wd_ref[0]) for k in range(n_part)]
        for k in range(n_part):
            _rows_to_tiles(ys_ref.at[rows[k]], y[k])

    @pl.when(pl.program_id(0) >= nv_ref[0])
    def _():
        ys_ref[...] = jnp.zeros_like(ys_ref)


def _experts(te, nv, xs, wg, wu, wd, tile):
    E, D, H = wg.shape
    blk = (tile * ROW_SUB, LANES)

    def row_map(j, te, nv):
        return (j, 0)

    return pl.pallas_call(
        _experts_kernel,
        grid_spec=pltpu.PrefetchScalarGridSpec(
            num_scalar_prefetch=2,
            grid=(xs.shape[0] // blk[0],),
            in_specs=[pl.BlockSpec(blk, row_map),
                      pl.BlockSpec((1, D, H), lambda j, te, nv: (te[j], 0, 0)),
                      pl.BlockSpec((1, D, H), lambda j, te, nv: (te[j], 0, 0)),
                      pl.BlockSpec((1, H, D), lambda j, te, nv: (te[j], 0, 0))],
            out_specs=pl.BlockSpec(blk, row_map),
        ),
        out_shape=jax.ShapeDtypeStruct(xs.shape, F32),
        compiler_params=_cparams(("arbitrary",)),
        name="moe_experts",
    )(te, nv, xs, wg, wu, wd)


def _combine_kernel(dcur_ref, dnxt_ref, ys_hbm, h_ref, rt_ref, fin_ref, o_ref, buf, sem, *, tm):
    i = pl.program_id(0)
    n = pl.num_programs(0)
    slot = i % 2

    def issue(d_ref, sl):
        def body(r, c):
            for s in range(2):
                pltpu.make_async_copy(_row_tile(ys_hbm, d_ref[0, s, r]), _row_tile(buf.at[sl, s], r),
                                      sem.at[sl]).start(priority=s)
            return c
        lax.fori_loop(0, tm, body, 0, unroll=8)

    @pl.when(i == 0)
    def _():
        issue(dcur_ref, 0)

    @pl.when(i + 1 < n)
    def _():
        issue(dnxt_ref, 1 - slot)

    def drain(r, c):
        for s in range(2):
            pltpu.make_async_copy(_row_tile(ys_hbm, 0), _row_tile(buf.at[slot, s], 0), sem.at[slot]).wait()
        return c

    lax.fori_loop(0, tm, drain, 0, unroll=8)
    y = (h_ref[...] + rt_ref[:, 0:1] * _tiles_to_rows(buf.at[slot, 0])
         + rt_ref[:, 1:2] * _tiles_to_rows(buf.at[slot, 1]))
    ms = jnp.mean(y * y, axis=-1, keepdims=True)
    o_ref[...] = (y * lax.rsqrt(ms + NORM_EPS)) * fin_ref[...]


def _combine(dest3, ys, h, rt, final_norm):
    nt, _, tm = dest3.shape
    T, D = h.shape
    return pl.pallas_call(
        functools.partial(_combine_kernel, tm=tm),
        grid=(nt,),
        in_specs=[pl.BlockSpec((1, 2, tm), lambda i: (i, 0, 0), memory_space=pltpu.SMEM),
                  pl.BlockSpec((1, 2, tm), lambda i: (jnp.minimum(i + 1, nt - 1), 0, 0), memory_space=pltpu.SMEM),
                  pl.BlockSpec(memory_space=pl.ANY),
                  pl.BlockSpec((tm, D), lambda i: (i, 0)),
                  pl.BlockSpec((tm, LANES), lambda i: (i, 0)),
                  pl.BlockSpec((1, D), lambda i: (0, 0))],
        out_specs=pl.BlockSpec((tm, D), lambda i: (i, 0)),
        out_shape=jax.ShapeDtypeStruct((T, D), F32),
        scratch_shapes=[pltpu.VMEM((2, 2, tm * ROW_SUB, LANES), F32), pltpu.SemaphoreType.DMA((2,))],
        compiler_params=_cparams(("arbitrary",)),
        name="moe_combine",
    )(dest3, dest3, ys, h, rt, final_norm)


def kernel(x, attn_norm, w_in, hg_lb_logits, hg_norm, fox_f_bias, fox_norm, w_out, ffn_norm,
           w_group, b_group, w_expert, b_expert, w_gate, w_up, w_down, final_norm):
    B, S, D = x.shape
    T = B * S
    assert w_in.shape[0] == 1, "single-layer block"
    hw = HG_HEADS * HG_DK
    fw = FOX_HEADS * FOX_DH
    wi = w_in[0]
    o = [0, hw, 2 * hw, 3 * hw, 4 * hw, 4 * hw + fw, 4 * hw + 2 * fw, 4 * hw + 3 * fw]
    w_all = jnp.concatenate(
        [wi[:, o[0]:o[1]], wi[:, o[2]:o[3]], wi[:, o[3]:o[4]], wi[:, o[4]:o[7]], wi[:, o[1]:o[2]],
         jnp.pad(wi[:, o[7]:], ((0, 0), (0, LANES - FOX_HEADS)))], axis=1).astype(BF16)

    x2 = x.reshape(T, D)
    tm_in = min(512, S)
    u, hgf, foxf = _inproj(x2, attn_norm.reshape(1, D), w_all, B, S, tm_in)

    cum = _foxcum(foxf.reshape(B * FOX_HEADS, S), jnp.tile(fox_f_bias[0], B).reshape(B * FOX_HEADS, 1),
                  min(256, S))
    tq = min(256, S)
    u3 = u.reshape(B, S, -1)
    o_b = _foxt(u3, cum.reshape(B, FOX_HEADS // 2, 2, S // tq, tq),
               jnp.tile(fox_norm[0], 2).reshape(1, LANES), tq)
    o_a = _hgrn2w(u3, hgf.reshape(B, S, hw), hg_lb_logits, jnp.tile(hg_norm[0], HG_HEADS).reshape(1, hw))

    wr = jnp.pad(jnp.concatenate([w_group[0], w_expert[0]], axis=1),
                 ((0, 0), (0, LANES - N_GROUPS - N_EXPERTS)))
    br = jnp.pad(jnp.concatenate([b_group[0], b_expert[0]]), (0, LANES - N_GROUPS - N_EXPERTS)).reshape(1, LANES)
    h, hn3, rt, r8 = _outproj(o_a.reshape(T, hw), o_b.reshape(T, fw), x2, w_out[0].astype(BF16),
                              ffn_norm[0].reshape(1, D), wr, br, min(512, T))

    blk = min(1024, T)
    n_rows = 2 * T + N_EXPERTS * MOE_TILE
    n_tiles = n_rows // MOE_TILE
    rank, cnt = _rank(r8, blk)
    dest, meta, te8 = _dest(r8, rank, cnt, blk, MOE_TILE, -(-n_tiles // LANES) * LANES)
    off, offe = meta[:, 0], meta[:, 1]
    te, nv = te8[0, :n_tiles], te8[1, :1]

    def tiles_of(tm):
        return dest[0:2].reshape(2, T // tm, tm).transpose(1, 0, 2)

    xs = _dispatch(off, offe, tiles_of(min(DISP_TM, T)), hn3, n_rows, MOE_TILE)
    ys = _experts(te, nv, xs, w_gate[0].astype(BF16), w_up[0].astype(BF16), w_down[0].astype(BF16), MOE_TILE)
    out = _combine(tiles_of(min(COMB_TM, T)), ys, h, rt, final_norm.reshape(1, D))
    return out.reshape(B, S, D)
```

```python
import functools

import jax
import jax.numpy as jnp
from jax import lax
from jax.experimental import pallas as pl
from jax.experimental.pallas import tpu as pltpu

F32 = jnp.float32
BF16 = jnp.bfloat16

NORM_EPS = 1e-6
NEG_INF = -1e30

LANES = 128
HG_HEADS = 4
HG_DK = 128
FOX_HEADS = 8
FOX_DH = 64
N_GROUPS = 4
PER_GROUP = 4
N_EXPERTS = 16

SUB = 16
SUPER = 128
HG_STEP = 2
MOE_TILE = 512
DISP_TM = 512
COMB_TM = 256
VMEM_LIMIT = 56 * 1024 * 1024


def _cparams(sem):
    return pltpu.CompilerParams(dimension_semantics=sem, vmem_limit_bytes=VMEM_LIMIT)


def _split3(x):
    hi = x.astype(BF16)
    r1 = x - hi.astype(F32)
    mid = r1.astype(BF16)
    lo = (r1 - mid.astype(F32)).astype(BF16)
    return hi, mid, lo


def _dot(a, b):
    return jnp.dot(a, b, preferred_element_type=F32)


def _dot_nt(a, b):
    return lax.dot_general(a, b, (((1,), (1,)), ((), ())), preferred_element_type=F32)


def _dot3(m_bf16, x_f32):
    hi, mid, lo = _split3(x_f32)
    return _dot(m_bf16, hi) + _dot(m_bf16, mid) + _dot(m_bf16, lo)


def _inproj_kernel(x_ref, g_ref, w_ref, u_ref, hgf_ref, foxf_ref, xn_ref, *, n_main, col_chunk):
    x = x_ref[...]
    ms = jnp.mean(x * x, axis=-1, keepdims=True)
    xn_ref[...] = ((x * lax.rsqrt(ms + NORM_EPS)) * g_ref[...]).astype(BF16)
    for j in range(n_main // col_chunk):
        r = _dot(xn_ref[...], w_ref[:, j * col_chunk:(j + 1) * col_chunk])
        u_ref[:, j * col_chunk:(j + 1) * col_chunk] = r.astype(BF16)
    r = _dot(xn_ref[...], w_ref[:, n_main:n_main + 4 * LANES])
    hgf_ref[...] = r
    r = _dot(xn_ref[...], w_ref[:, n_main + 4 * LANES:n_main + 5 * LANES])
    foxf_ref[0] = r.T[0:FOX_HEADS, :]


def _inproj(x2, attn_norm, w_all, B, S, tm):
    T, D = x2.shape
    n_main = 6 * 512
    per_seq = S // tm
    return pl.pallas_call(
        functools.partial(_inproj_kernel, n_main=n_main, col_chunk=512),
        grid=(T // tm,),
        in_specs=[
            pl.BlockSpec((tm, D), lambda i: (i, 0)),
            pl.BlockSpec((1, D), lambda i: (0, 0)),
            pl.BlockSpec(w_all.shape, lambda i: (0, 0)),
        ],
        out_specs=[
            pl.BlockSpec((tm, n_main), lambda i: (i, 0)),
            pl.BlockSpec((tm, 4 * LANES), lambda i: (i, 0)),
            pl.BlockSpec((1, FOX_HEADS, tm), lambda i: (i // per_seq, 0, i % per_seq)),
        ],
        out_shape=[
            jax.ShapeDtypeStruct((T, n_main), BF16),
            jax.ShapeDtypeStruct((T, 4 * LANES), F32),
            jax.ShapeDtypeStruct((B, FOX_HEADS, S), F32),
        ],
        scratch_shapes=[pltpu.VMEM((tm, D), BF16)],
        compiler_params=_cparams(("arbitrary",)),
        name="inproj",
    )(x2, attn_norm, w_all)


def _foxcum_kernel(f_ref, bias_ref, cum_ref, off_ref):
    blk = f_ref.shape[1]

    @pl.when(pl.program_id(0) == 0)
    def _():
        off_ref[...] = jnp.zeros_like(off_ref)

    r_i = lax.broadcasted_iota(jnp.int32, (blk, blk), 0)
    c_i = lax.broadcasted_iota(jnp.int32, (blk, blk), 1)
    upper = (r_i <= c_i).astype(BF16)
    z = f_ref[...] + bias_ref[...]
    logf = jnp.minimum(z, 0.0) - jnp.log(1.0 + jnp.exp(-jnp.abs(z)))
    hi, mid, lo = _split3(logf)
    cum = _dot(hi, upper) + _dot(mid, upper) + _dot(lo, upper) + off_ref[...]
    cum_ref[...] = cum
    off_ref[...] = cum[:, blk - 1:blk]


def _foxcum(foxf2, bias_col, blk):
    R, S = foxf2.shape
    return pl.pallas_call(
        _foxcum_kernel,
        grid=(S // blk,),
        in_specs=[pl.BlockSpec((R, blk), lambda j: (0, j)), pl.BlockSpec((R, 1), lambda j: (0, 0))],
        out_specs=pl.BlockSpec((R, blk), lambda j: (0, j)),
        out_shape=jax.ShapeDtypeStruct((R, S), F32),
        scratch_shapes=[pltpu.VMEM((R, 1), F32)],
        compiler_params=_cparams(("arbitrary",)),
        name="foxcum",
    )(foxf2, bias_col)


def _foxt_kernel(q_ref, k_ref, v_ref, cum_ref, gain_ref, o_ref,
                 kaug_scr, vt_scr, qa_scr, s_scr, p_scr, m_scr, al_scr, acc_scr, *, tq):
    qi = pl.program_id(1)
    nkb = vt_scr.shape[1]
    lane = lax.broadcasted_iota(jnp.int32, (1, LANES), 1)
    rid = lax.broadcasted_iota(jnp.int32, (LANES, 1), 0)
    lo_lane, lo_row = lane < FOX_DH, rid < FOX_DH
    aug_lane = [(lane >= FOX_DH) & (lane < FOX_DH + 3), lane < 3]

    @pl.when(qi == 0)
    def _():
        for kb in range(nkb):
            rows = slice(kb * tq, (kb + 1) * tq)
            for p in range(FOX_HEADS // 2):
                ls = slice(p * LANES, (p + 1) * LANES)
                pe = _split3(cum_ref[0, p, 0, kb:kb + 1, :])
                po = _split3(cum_ref[0, p, 1, kb:kb + 1, :])
                x = jnp.zeros((LANES, tq), F32)
                for j in range(3):
                    x = jnp.where(rid == FOX_DH + j, pe[j].astype(F32), x)
                    x = jnp.where(rid == j, po[j].astype(F32), x)
                aug = x.T
                kf = k_ref[0, rows, ls].astype(F32)
                kaug_scr[2 * p, rows, :] = jnp.where(lo_lane, kf, aug).astype(BF16)
                kaug_scr[2 * p + 1, rows, :] = jnp.where(lo_lane, aug, kf).astype(BF16)
                vt = v_ref[0, rows, ls].astype(F32).T
                vt_scr[2 * p, kb] = jnp.where(lo_row, vt, 1.0).astype(BF16)
                vt_scr[2 * p + 1, kb] = jnp.where(lo_row, 1.0, vt).astype(BF16)

    for p in range(FOX_HEADS // 2):
        q = q_ref[0, :, p * LANES:(p + 1) * LANES].astype(F32) * (FOX_DH ** -0.5)
        qa_scr[2 * p] = jnp.where(lo_lane, q, jnp.where(aug_lane[0], -1.0, 0.0)).astype(BF16)
        qa_scr[2 * p + 1] = jnp.where(lo_lane, jnp.where(aug_lane[1], -1.0, 0.0), q).astype(BF16)
    m_scr[...] = jnp.full(m_scr.shape, NEG_INF, F32)
    acc_scr[...] = jnp.zeros_like(acc_scr)
    krow = lax.broadcasted_iota(jnp.int32, (tq, LANES), 0)
    qcol = lax.broadcasted_iota(jnp.int32, (tq, LANES), 1)

    def step(kb, masked):
        r0 = pl.multiple_of(kb * tq, tq)
        for hd in range(FOX_HEADS):
            s_scr[hd] = _dot_nt(kaug_scr[hd, pl.ds(r0, tq), :], qa_scr[hd])
        for hd in range(FOX_HEADS):
            for lt in range(tq // LANES):
                ls = slice(lt * LANES, (lt + 1) * LANES)
                s = s_scr[hd, :, ls]
                if masked:
                    s = jnp.where(krow <= qcol + lt * LANES, s, NEG_INF)
                m = m_scr[hd:hd + 1, ls]
                m_new = jnp.maximum(m, jnp.max(s, axis=0, keepdims=True))
                p_scr[hd, :, ls] = jnp.exp(s - m_new).astype(BF16)
                al_scr[hd:hd + 1, ls] = jnp.exp(m - m_new)
                m_scr[hd:hd + 1, ls] = m_new
        for hd in range(FOX_HEADS):
            acc_scr[hd] = acc_scr[hd] * al_scr[hd:hd + 1, :] + _dot(vt_scr[hd, kb], p_scr[hd])

    def body(kb, c):
        step(kb, False)
        return c

    lax.fori_loop(0, qi, body, 0)
    step(qi, True)
    for p in range(FOX_HEADS // 2):
        ae, ao = acc_scr[2 * p], acc_scr[2 * p + 1]
        o = jnp.where(lo_row, ae / ae[FOX_DH:FOX_DH + 1, :], ao / ao[0:1, :])
        o2 = o * o
        ms_e = jnp.sum(jnp.where(lo_row, o2, 0.0), axis=0, keepdims=True) * (1.0 / FOX_DH)
        ms_o = jnp.sum(jnp.where(lo_row, 0.0, o2), axis=0, keepdims=True) * (1.0 / FOX_DH)
        o = o * lax.rsqrt(jnp.where(lo_row, ms_e, ms_o) + NORM_EPS)
        o_ref[0, :, p * LANES:(p + 1) * LANES] = (o.T * gain_ref[...]).astype(BF16)


def _foxt(u3, cum5, gain2, tq):
    B, S, _ = u3.shape
    nq = S // tq
    fw = FOX_HEADS * FOX_DH
    qc, kc, vc = 3, 4, 5
    return pl.pallas_call(
        functools.partial(_foxt_kernel, tq=tq),
        grid=(B, nq),
        in_specs=[
            pl.BlockSpec((1, tq, fw), lambda b, i: (b, i, qc)),
            pl.BlockSpec((1, S, fw), lambda b, i: (b, 0, kc)),
            pl.BlockSpec((1, S, fw), lambda b, i: (b, 0, vc)),
            pl.BlockSpec((1, FOX_HEADS // 2, 2, nq, tq), lambda b, i: (b, 0, 0, 0, 0)),
            pl.BlockSpec((1, LANES), lambda b, i: (0, 0)),
        ],
        out_specs=pl.BlockSpec((1, tq, fw), lambda b, i: (b, i, 0)),
        out_shape=jax.ShapeDtypeStruct((B, S, fw), BF16),
        scratch_shapes=[
            pltpu.VMEM((FOX_HEADS, S, LANES), BF16),
            pltpu.VMEM((FOX_HEADS, nq, LANES, tq), BF16),
            pltpu.VMEM((FOX_HEADS, tq, LANES), BF16),
            pltpu.VMEM((FOX_HEADS, tq, tq), F32),
            pltpu.VMEM((FOX_HEADS, tq, tq), BF16),
            pltpu.VMEM((FOX_HEADS, tq), F32),
            pltpu.VMEM((FOX_HEADS, tq), F32),
            pltpu.VMEM((FOX_HEADS, LANES, tq), F32),
        ],
        compiler_params=_cparams(("arbitrary", "arbitrary")),
        name="fox",
    )(u3, u3, u3, cum5, gain2)


def _hgrn2w_kernel(q_ref, i_ref, g_ref, f_ref, lbl_ref, gain_ref, o_ref, b_scr, q_scr, c_scr, st_scr):
    S, W = q_ref.shape[1], q_ref.shape[2]
    nh = W // LANES
    rows_step = HG_STEP * SUPER
    n_sub = SUPER // SUB
    hsl = [slice(h * LANES, (h + 1) * LANES) for h in range(nh)]
    a = lbl_ref[...]
    am = jnp.max(a, axis=0, keepdims=True)
    ea = jnp.exp(a - am)
    lb = ea[0:1, :] / (ea[0:1, :] + ea[1:2, :])

    r_i = lax.broadcasted_iota(jnp.int32, (SUPER, SUPER), 0)
    c_i = lax.broadcasted_iota(jnp.int32, (SUPER, SUPER), 1)
    same = (r_i // SUB) == (c_i // SUB)
    tri = (same & (c_i <= r_i)).astype(BF16)
    blk1 = same.astype(BF16)
    ones = jnp.ones((LANES, LANES), BF16)
    lane = lax.broadcasted_iota(jnp.int32, (1, LANES), 1)
    trow = lax.broadcasted_iota(jnp.int32, (8, W), 0)

    st_scr[...] = jnp.zeros_like(st_scr)

    def superblock(sb, _):
        r0 = pl.multiple_of(sb * rows_step, rows_step)
        hq = 0.5 * q_ref[0, pl.ds(r0, rows_step), :].astype(F32)
        q = hq + hq * jnp.tanh(hq)
        f = lb + (1.0 - lb) * (0.5 + 0.5 * jnp.tanh(0.5 * f_ref[0, pl.ds(r0, rows_step), :]))
        lf2 = jnp.log2(f)
        halves_r = [slice(k * SUPER, (k + 1) * SUPER) for k in range(HG_STEP)]
        b2 = jnp.concatenate([_dot3(tri, lf2[hr]) for hr in halves_r], axis=0)
        bt2 = jnp.concatenate([_dot3(blk1, lf2[hr]) for hr in halves_r], axis=0)
        c2 = b2 - jnp.log2(1.0 - f)
        qt = (q * jnp.exp2(b2)).astype(BF16)
        kh = jnp.exp2(bt2 - c2).astype(BF16)
        gam = jnp.exp2(bt2)
        b_scr[...] = b2
        q_scr[...] = q
        c_scr[...] = c2
        v = i_ref[0, pl.ds(r0, rows_step), :]

        st = [st_scr[h] for h in range(nh)]
        o_inter = [[] for _ in range(nh)]
        for c in range(HG_STEP * n_sub):
            rows = slice(c * SUB, (c + 1) * SUB)
            for h in range(nh):
                o_inter[h].append(_dot_nt(qt[rows, hsl[h]], st[h].astype(BF16)))
                dst = lax.dot_general(v[rows, hsl[h]], kh[rows, hsl[h]], (((0,), (0,)), ((), ())),
                                      preferred_element_type=F32)
                st[h] = gam[c * SUB:c * SUB + 1, hsl[h]] * st[h] + dst
        for h in range(nh):
            st_scr[h] = st[h]

        a_rows = [[] for _ in range(nh)]
        for c in range(HG_STEP * n_sub):
            base = c * SUB
            col0 = base % SUPER
            tiles = []
            lo_half = []
            for j in range(SUB):
                cs = c_scr[base + j:base + j + 1, :]
                halves = []
                for hf in range(2):
                    if hf == 0 and j >= 8:
                        continue
                    t0 = base + 8 * hf
                    p = q_scr[t0:t0 + 8, :] * jnp.exp2(b_scr[t0:t0 + 8, :] - cs)
                    if j // 8 == hf:
                        p = jnp.where(trow >= (j % 8), p, 0.0)
                    halves.append(p)
                if j < 8:
                    tiles.append(jnp.concatenate(halves, axis=0))
                else:
                    lo_half.append(halves[0])
            for m in range(4):
                tiles.append(jnp.concatenate([lo_half[2 * m], lo_half[2 * m + 1]], axis=0))
            pc = jnp.concatenate(tiles, axis=0).astype(BF16)
            for h in range(nh):
                rc = _dot(pc[:, hsl[h]], ones)
                a_top = jnp.zeros((8, LANES), F32)
                a_bot = jnp.zeros((8, LANES), F32)
                for j in range(8):
                    sel = lane == (col0 + j)
                    a_top = jnp.where(sel, rc[16 * j:16 * j + 8, :], a_top)
                    a_bot = jnp.where(sel, rc[16 * j + 8:16 * j + 16, :], a_bot)
                for j in range(8, SUB):
                    sel = lane == (col0 + j)
                    a_bot = jnp.where(sel, rc[128 + 8 * (j - 8):128 + 8 * (j - 8) + 8, :], a_bot)
                a_rows[h] += [a_top, a_bot]

        outs = []
        for h in range(nh):
            parts = []
            for k, hr in enumerate(halves_r):
                tl = a_rows[h][k * 2 * n_sub:(k + 1) * 2 * n_sub]
                a_blk = jnp.concatenate(tl, axis=0).astype(BF16)
                parts.append(_dot(a_blk, v[hr, hsl[h]]))
            o = jnp.concatenate(o_inter[h], axis=0) + jnp.concatenate(parts, axis=0)
            ms = jnp.mean(o * o, axis=-1, keepdims=True)
            outs.append(o * lax.rsqrt(ms + NORM_EPS))
        o = jnp.concatenate(outs, axis=1) * gain_ref[...]
        hg = 0.5 * g_ref[0, pl.ds(r0, rows_step), :].astype(F32)
        o_ref[0, pl.ds(r0, rows_step), :] = (o * (hg + hg * jnp.tanh(hg))).astype(BF16)
        return 0

    lax.fori_loop(0, S // rows_step, superblock, 0)


def _hgrn2w(u3, hgf3, lb_logits, gain_w):
    B, S, _ = u3.shape
    W = HG_HEADS * HG_DK
    return pl.pallas_call(
        _hgrn2w_kernel,
        grid=(B,),
        in_specs=[
            pl.BlockSpec((1, S, W), lambda b: (b, 0, 0)),
            pl.BlockSpec((1, S, W), lambda b: (b, 0, 1)),
            pl.BlockSpec((1, S, W), lambda b: (b, 0, 2)),
            pl.BlockSpec((1, S, W), lambda b: (b, 0, 0)),
            pl.BlockSpec((2, W), lambda b: (0, 0)),
            pl.BlockSpec((1, W), lambda b: (0, 0)),
        ],
        out_specs=pl.BlockSpec((1, S, W), lambda b: (b, 0, 0)),
        out_shape=jax.ShapeDtypeStruct((B, S, W), BF16),
        scratch_shapes=[pltpu.VMEM((HG_STEP * SUPER, W), F32)] * 3 + [pltpu.VMEM((HG_HEADS, LANES, LANES), F32)],
        compiler_params=_cparams(("arbitrary",)),
        name="hgrn2",
    )(u3, u3, u3, hgf3, lb_logits, gain_w)


ROW_SUB = 8


def _rows_to_tiles(ref, x):
    m = x.shape[0]
    for c in range(ROW_SUB):
        ref[pl.ds(c, m, stride=ROW_SUB), :] = x[:, c * LANES:(c + 1) * LANES]


def _tiles_to_rows(ref):
    m = ref.shape[0] // ROW_SUB
    return jnp.concatenate([ref[pl.ds(c, m, stride=ROW_SUB), :] for c in range(ROW_SUB)], axis=1)


def _row_tile(ref, r):
    return ref.at[pl.ds(pl.multiple_of(r * ROW_SUB, ROW_SUB), ROW_SUB)]


def _outproj_kernel(oa_ref, ob_ref, x_ref, wo_ref, g_ref, wr_ref, br_ref, h_ref, hn_ref, rt_ref, r8_ref):
    half = oa_ref.shape[1]
    h = x_ref[...] + _dot(oa_ref[...], wo_ref[0:half, :]) + _dot(ob_ref[...], wo_ref[half:2 * half, :])
    h_ref[...] = h
    ms = jnp.mean(h * h, axis=-1, keepdims=True)
    hn = (h * lax.rsqrt(ms + NORM_EPS)) * g_ref[...]
    _rows_to_tiles(hn_ref, hn)
    hn_hi = hn.astype(BF16)
    hn_lo = (hn - hn_hi.astype(F32)).astype(BF16)
    wr = wr_ref[...]
    wr_hi = wr.astype(BF16)
    wr_lo = (wr - wr_hi.astype(F32)).astype(BF16)
    logits = _dot(hn_hi, wr_hi) + _dot(hn_lo, wr_hi) + _dot(hn_hi, wr_lo) + br_ref[...]
    lt = logits.T
    g = [lt[i:i + 1, :] for i in range(N_GROUPS)]
    gm = jnp.maximum(jnp.maximum(g[0], g[1]), jnp.maximum(g[2], g[3]))
    gsel = jnp.where(g[0] == gm, 0, jnp.where(g[1] == gm, 1, jnp.where(g[2] == gm, 2, 3)))
    pg = 1.0 / (jnp.exp(g[0] - gm) + jnp.exp(g[1] - gm) + jnp.exp(g[2] - gm) + jnp.exp(g[3] - gm))
    e = []
    for i in range(PER_GROUP):
        rows = [lt[N_GROUPS + PER_GROUP * gg + i:N_GROUPS + PER_GROUP * gg + i + 1, :] for gg in range(N_GROUPS)]
        e.append(jnp.where(gsel == 0, rows[0], jnp.where(gsel == 1, rows[1], jnp.where(gsel == 2, rows[2], rows[3]))))
    e1 = jnp.maximum(jnp.maximum(e[0], e[1]), jnp.maximum(e[2], e[3]))
    i1 = jnp.where(e[0] == e1, 0, jnp.where(e[1] == e1, 1, jnp.where(e[2] == e1, 2, 3)))
    ex = [jnp.where(i1 == i, -jnp.inf, e[i]) for i in range(PER_GROUP)]
    e2 = jnp.maximum(jnp.maximum(ex[0], ex[1]), jnp.maximum(ex[2], ex[3]))
    i2 = jnp.where(ex[0] == e2, 0, jnp.where(ex[1] == e2, 1, jnp.where(ex[2] == e2, 2, 3)))
    r = jnp.exp(e2 - e1)
    w1 = pg / (1.0 + r)
    w2 = w1 * r
    x1 = gsel * PER_GROUP + i1
    x2 = gsel * PER_GROUP + i2
    ridx = lax.broadcasted_iota(jnp.int32, (LANES, 1), 0)
    wt = jnp.where(ridx == 0, w1, 0.0) + jnp.where(ridx == 1, w2, 0.0)
    rt_ref[...] = wt.T
    sidx = lax.broadcasted_iota(jnp.int32, (8, 1), 0)
    r8_ref[...] = (jnp.where(sidx == 0, x1, 0) + jnp.where(sidx == 1, x2, 0)).astype(F32)


def _outproj(oa, ob, x2, w_out, ffn_norm, wr, br, tm):
    T, D = x2.shape
    half = oa.shape[1]
    return pl.pallas_call(
        _outproj_kernel,
        grid=(T // tm,),
        in_specs=[
            pl.BlockSpec((tm, half), lambda i: (i, 0)),
            pl.BlockSpec((tm, half), lambda i: (i, 0)),
            pl.BlockSpec((tm, D), lambda i: (i, 0)),
            pl.BlockSpec(w_out.shape, lambda i: (0, 0)),
            pl.BlockSpec((1, D), lambda i: (0, 0)),
            pl.BlockSpec(wr.shape, lambda i: (0, 0)),
            pl.BlockSpec((1, LANES), lambda i: (0, 0)),
        ],
        out_specs=[
            pl.BlockSpec((tm, D), lambda i: (i, 0)),
            pl.BlockSpec((tm * ROW_SUB, LANES), lambda i: (i, 0)),
            pl.BlockSpec((tm, LANES), lambda i: (i, 0)),
            pl.BlockSpec((8, tm), lambda i: (0, i)),
        ],
        out_shape=[
            jax.ShapeDtypeStruct((T, D), F32),
            jax.ShapeDtypeStruct((T * ROW_SUB, LANES), F32),
            jax.ShapeDtypeStruct((T, LANES), F32),
            jax.ShapeDtypeStruct((8, T), F32),
        ],
        compiler_params=_cparams(("arbitrary",)),
        name="outproj",
    )(oa, ob, x2, w_out, ffn_norm, wr, br)


def _rank_kernel(r8_ref, rank_ref, cnt_ref, off_scr):
    blk = r8_ref.shape[1]

    @pl.when(pl.program_id(0) == 0)
    def _():
        off_scr[...] = jnp.zeros_like(off_scr)

    x1, x2 = r8_ref[0:1, :], r8_ref[1:2, :]
    eid = lax.broadcasted_iota(jnp.int32, (N_EXPERTS, 1), 0).astype(F32)
    r_i = lax.broadcasted_iota(jnp.int32, (blk, blk), 0)
    c_i = lax.broadcasted_iota(jnp.int32, (blk, blk), 1)
    upper = (r_i <= c_i).astype(BF16)
    hit = ((x1 == eid) | (x2 == eid)).astype(BF16)
    cum = _dot(hit, upper) + off_scr[...]
    rank1 = jnp.sum(jnp.where(x1 == eid, cum, 0.0), axis=0, keepdims=True) - 1.0
    rank2 = jnp.sum(jnp.where(x2 == eid, cum, 0.0), axis=0, keepdims=True) - 1.0
    sidx = lax.broadcasted_iota(jnp.int32, (8, 1), 0)
    rank_ref[...] = jnp.where(sidx == 0, rank1, 0.0) + jnp.where(sidx == 1, rank2, 0.0)
    off_scr[...] = cum[:, blk - 1:blk]
    cnt_ref[...] = jnp.broadcast_to(cum[:, blk - 1:blk], cnt_ref.shape)


def _rank(r8, blk):
    T = r8.shape[1]
    return pl.pallas_call(
        _rank_kernel,
        grid=(T // blk,),
        in_specs=[pl.BlockSpec((8, blk), lambda j: (0, j))],
        out_specs=[pl.BlockSpec((8, blk), lambda j: (0, j)), pl.BlockSpec((N_EXPERTS, LANES), lambda j: (0, 0))],
        out_shape=[jax.ShapeDtypeStruct((8, T), F32), jax.ShapeDtypeStruct((N_EXPERTS, LANES), F32)],
        scratch_shapes=[pltpu.VMEM((N_EXPERTS, 1), F32)],
        compiler_params=_cparams(("arbitrary",)),
        name="moe_rank",
    )(r8)


def _dest_kernel(r8_ref, rank_ref, cnt_ref, dest_ref, meta_ref, te_ref, *, tile):
    cnt = cnt_ref[...]
    cntp = jnp.floor((cnt + (tile - 1.0)) * (1.0 / tile)) * tile
    r_i = lax.broadcasted_iota(jnp.int32, (N_EXPERTS, N_EXPERTS), 0)
    c_i = lax.broadcasted_iota(jnp.int32, (N_EXPERTS, N_EXPERTS), 1)
    off = _dot3((c_i < r_i).astype(BF16), cntp)
    offe = off + cntp
    x1, x2 = r8_ref[0:1, :], r8_ref[1:2, :]
    eid = lax.broadcasted_iota(jnp.int32, (N_EXPERTS, 1), 0).astype(F32)
    off_c = off[:, 0:1]
    d1 = jnp.sum(jnp.where(x1 == eid, off_c, 0.0), axis=0, keepdims=True) + rank_ref[0:1, :]
    d2 = jnp.sum(jnp.where(x2 == eid, off_c, 0.0), axis=0, keepdims=True) + rank_ref[1:2, :]
    sidx = lax.broadcasted_iota(jnp.int32, (8, 1), 0)
    dest_ref[...] = (jnp.where(sidx == 0, d1, 0.0) + jnp.where(sidx == 1, d2, 0.0)).astype(jnp.int32)
    lane = lax.broadcasted_iota(jnp.int32, (1, LANES), 1)
    meta_ref[...] = jnp.where(lane == 0, off, offe).astype(jnp.int32)
    start = lax.broadcasted_iota(jnp.int32, (1, te_ref.shape[1]), 1).astype(F32) * tile
    te = jnp.sum((start >= offe[:, 0:1]).astype(F32), axis=0, keepdims=True)
    nvalid = offe[N_EXPERTS - 1:N_EXPERTS, 0:1] * (1.0 / tile)
    te_ref[...] = (jnp.where(sidx == 0, jnp.minimum(te, N_EXPERTS - 1.0), 0.0)
                   + jnp.where(sidx == 1, nvalid, 0.0)).astype(jnp.int32)


def _dest(r8, rank, cnt, blk, tile, n_tiles_pad):
    T = r8.shape[1]
    return pl.pallas_call(
        functools.partial(_dest_kernel, tile=float(tile)),
        grid=(T // blk,),
        in_specs=[pl.BlockSpec((8, blk), lambda j: (0, j)), pl.BlockSpec((8, blk), lambda j: (0, j)),
                  pl.BlockSpec((N_EXPERTS, LANES), lambda j: (0, 0))],
        out_specs=[pl.BlockSpec((8, blk), lambda j: (0, j)), pl.BlockSpec((N_EXPERTS, LANES), lambda j: (0, 0)),
                   pl.BlockSpec((8, n_tiles_pad), lambda j: (0, 0))],
        out_shape=[jax.ShapeDtypeStruct((8, T), jnp.int32), jax.ShapeDtypeStruct((N_EXPERTS, LANES), jnp.int32),
                   jax.ShapeDtypeStruct((8, n_tiles_pad), jnp.int32)],
        compiler_params=_cparams(("arbitrary",)),
        name="moe_dest",
    )(r8, rank, cnt)


def _dispatch_kernel(off_ref, offe_ref, dest_ref, hn_ref, xs_hbm, zero_scr, sem, zsem, *, tm, tile):
    i = pl.program_id(0)

    @pl.when(i == 0)
    def _():
        zero_scr[...] = jnp.zeros_like(zero_scr)
        for e in range(N_EXPERTS):
            @pl.when(offe_ref[e] > off_ref[e])
            def _():
                dst = xs_hbm.at[pl.ds(pl.multiple_of((offe_ref[e] - tile) * ROW_SUB, ROW_SUB), tile * ROW_SUB)]
                c = pltpu.make_async_copy(zero_scr, dst, zsem)
                c.start()
                c.wait()

        def fill_unused(j, c):
            dst = xs_hbm.at[pl.ds(pl.multiple_of(j * tile * ROW_SUB, ROW_SUB), tile * ROW_SUB)]
            cp = pltpu.make_async_copy(zero_scr, dst, zsem)
            cp.start()
            cp.wait()
            return c

        lax.fori_loop(offe_ref[N_EXPERTS - 1] // tile, xs_hbm.shape[0] // (tile * ROW_SUB), fill_unused, 0)

    def issue(r, c):
        for s in range(2):
            pltpu.make_async_copy(_row_tile(hn_ref, r), _row_tile(xs_hbm, dest_ref[0, s, r]), sem).start(priority=s)
        return c

    lax.fori_loop(0, tm, issue, 0, unroll=8)

    def drain(r, c):
        for s in range(2):
            pltpu.make_async_copy(_row_tile(hn_ref, 0), _row_tile(xs_hbm, 0), sem).wait()
        return c

    lax.fori_loop(0, tm, drain, 0, unroll=8)


def _dispatch(off, offe, dest3, hn3, n_rows, tile):
    nt, _, tm = dest3.shape
    return pl.pallas_call(
        functools.partial(_dispatch_kernel, tm=tm, tile=tile),
        grid_spec=pltpu.PrefetchScalarGridSpec(
            num_scalar_prefetch=2,
            grid=(nt,),
            in_specs=[pl.BlockSpec((1, 2, tm), lambda i, o, oe: (i, 0, 0), memory_space=pltpu.SMEM),
                      pl.BlockSpec((tm * ROW_SUB, LANES), lambda i, o, oe: (i, 0))],
            out_specs=pl.BlockSpec(memory_space=pl.ANY),
            scratch_shapes=[pltpu.VMEM((tile * ROW_SUB, LANES), F32), pltpu.SemaphoreType.DMA,
                            pltpu.SemaphoreType.DMA],
        ),
        out_shape=jax.ShapeDtypeStruct((n_rows * ROW_SUB, LANES), F32),
        compiler_params=_cparams(("arbitrary",)),
        name="moe_dispatch",
    )(off, offe, dest3, hn3)


def _experts_kernel(te_ref, nv_ref, xs_ref, wg_ref, wu_ref, wd_ref, ys_ref):
    @pl.when(pl.program_id(0) < nv_ref[0])
    def _():
        n_part = 2
        part = xs_ref.shape[0] // n_part
        rows = [pl.ds(k * part, part) for k in range(n_part)]
        x = [_tiles_to_rows(xs_ref.at[rows[k]]).astype(BF16) for k in range(n_part)]
        a = [_dot(x[k], wg_ref[0]) for k in range(n_part)]
        u = [_dot(x[k], wu_ref[0]) for k in range(n_part)]
        hid = [((a[k] * jax.nn.sigmoid(a[k])) * u[k]).astype(BF16) for k in range(n_part)]
        y = [_dot(hid[k], wd_ref[0]) for k in range(n_part)]
        for k in range(n_part):
            _rows_to_tiles(ys_ref.at[rows[k]], y[k])

    @pl.when(pl.program_id(0) >= nv_ref[0])
    def _():
        ys_ref[...] = jnp.zeros_like(ys_ref)


def _experts(te, nv, xs, wg, wu, wd, tile):
    E, D, H = wg.shape
    blk = (tile * ROW_SUB, LANES)

    def row_map(j, te, nv):
        return (j, 0)

    return pl.pallas_call(
        _experts_kernel,
        grid_spec=pltpu.PrefetchScalarGridSpec(
            num_scalar_prefetch=2,
            grid=(xs.shape[0] // blk[0],),
            in_specs=[pl.BlockSpec(blk, row_map),
                      pl.BlockSpec((1, D, H), lambda j, te, nv: (te[j], 0, 0)),
                      pl.BlockSpec((1, D, H), lambda j, te, nv: (te[j], 0, 0)),
                      pl.BlockSpec((1, H, D), lambda j, te, nv: (te[j], 0, 0))],
            out_specs=pl.BlockSpec(blk, row_map),
        ),
        out_shape=jax.ShapeDtypeStruct(xs.shape, F32),
        compiler_params=_cparams(("arbitrary",)),
        name="moe_experts",
    )(te, nv, xs, wg, wu, wd)


def _combine_kernel(dcur_ref, dnxt_ref, ys_hbm, h_ref, rt_ref, fin_ref, o_ref, buf, sem, *, tm):
    i = pl.program_id(0)
    n = pl.num_programs(0)
    slot = i % 2

    def issue(d_ref, sl):
        def body(r, c):
            for s in range(2):
                pltpu.make_async_copy(_row_tile(ys_hbm, d_ref[0, s, r]), _row_tile(buf.at[sl, s], r),
                                      sem.at[sl]).start(priority=s)
            return c
        lax.fori_loop(0, tm, body, 0, unroll=8)

    @pl.when(i == 0)
    def _():
        issue(dcur_ref, 0)

    @pl.when(i + 1 < n)
    def _():
        issue(dnxt_ref, 1 - slot)

    def drain(r, c):
        for s in range(2):
            pltpu.make_async_copy(_row_tile(ys_hbm, 0), _row_tile(buf.at[slot, s], 0), sem.at[slot]).wait()
        return c

    lax.fori_loop(0, tm, drain, 0, unroll=8)
    y = (h_ref[...] + rt_ref[:, 0:1] * _tiles_to_rows(buf.at[slot, 0])
         + rt_ref[:, 1:2] * _tiles_to_rows(buf.at[slot, 1]))
    ms = jnp.mean(y * y, axis=-1, keepdims=True)
    o_ref[...] = (y * lax.rsqrt(ms + NORM_EPS)) * fin_ref[...]


def _combine(dest3, ys, h, rt, final_norm):
    nt, _, tm = dest3.shape
    T, D = h.shape
    return pl.pallas_call(
        functools.partial(_combine_kernel, tm=tm),
        grid=(nt,),
        in_specs=[pl.BlockSpec((1, 2, tm), lambda i: (i, 0, 0), memory_space=pltpu.SMEM),
                  pl.BlockSpec((1, 2, tm), lambda i: (jnp.minimum(i + 1, nt - 1), 0, 0), memory_space=pltpu.SMEM),
                  pl.BlockSpec(memory_space=pl.ANY),
                  pl.BlockSpec((tm, D), lambda i: (i, 0)),
                  pl.BlockSpec((tm, LANES), lambda i: (i, 0)),
                  pl.BlockSpec((1, D), lambda i: (0, 0))],
        out_specs=pl.BlockSpec((tm, D), lambda i: (i, 0)),
        out_shape=jax.ShapeDtypeStruct((T, D), F32),
        scratch_shapes=[pltpu.VMEM((2, 2, tm * ROW_SUB, LANES), F32), pltpu.SemaphoreType.DMA((2,))],
        compiler_params=_cparams(("arbitrary",)),
        name="moe_combine",
    )(dest3, dest3, ys, h, rt, final_norm)


def kernel(x, attn_norm, w_in, hg_lb_logits, hg_norm, fox_f_bias, fox_norm, w_out, ffn_norm,
           w_group, b_group, w_expert, b_expert, w_gate, w_up, w_down, final_norm):
    B, S, D = x.shape
    T = B * S
    assert w_in.shape[0] == 1, "single-layer block"
    hw = HG_HEADS * HG_DK
    fw = FOX_HEADS * FOX_DH
    wi = w_in[0]
    o = [0, hw, 2 * hw, 3 * hw, 4 * hw, 4 * hw + fw, 4 * hw + 2 * fw, 4 * hw + 3 * fw]
    w_all = jnp.concatenate(
        [wi[:, o[0]:o[1]], wi[:, o[2]:o[3]], wi[:, o[3]:o[4]], wi[:, o[4]:o[7]], wi[:, o[1]:o[2]],
         jnp.pad(wi[:, o[7]:], ((0, 0), (0, LANES - FOX_HEADS)))], axis=1).astype(BF16)

    x2 = x.reshape(T, D)
    tm_in = min(512, S)
    u, hgf, foxf = _inproj(x2, attn_norm.reshape(1, D), w_all, B, S, tm_in)

    cum = _foxcum(foxf.reshape(B * FOX_HEADS, S), jnp.tile(fox_f_bias[0], B).reshape(B * FOX_HEADS, 1),
                  min(256, S))
    tq = min(256, S)
    u3 = u.reshape(B, S, -1)
    o_b = _foxt(u3, cum.reshape(B, FOX_HEADS // 2, 2, S // tq, tq),
               jnp.tile(fox_norm[0], 2).reshape(1, LANES), tq)
    o_a = _hgrn2w(u3, hgf.reshape(B, S, hw), hg_lb_logits, jnp.tile(hg_norm[0], HG_HEADS).reshape(1, hw))

    wr = jnp.pad(jnp.concatenate([w_group[0], w_expert[0]], axis=1),
                 ((0, 0), (0, LANES - N_GROUPS - N_EXPERTS)))
    br = jnp.pad(jnp.concatenate([b_group[0], b_expert[0]]), (0, LANES - N_GROUPS - N_EXPERTS)).reshape(1, LANES)
    h, hn3, rt, r8 = _outproj(o_a.reshape(T, hw), o_b.reshape(T, fw), x2, w_out[0].astype(BF16),
                              ffn_norm[0].reshape(1, D), wr, br, min(512, T))

    blk = min(1024, T)
    n_rows = 2 * T + N_EXPERTS * MOE_TILE
    n_tiles = n_rows // MOE_TILE
    rank, cnt = _rank(r8, blk)
    dest, meta, te8 = _dest(r8, rank, cnt, blk, MOE_TILE, -(-n_tiles // LANES) * LANES)
    off, offe = meta[:, 0], meta[:, 1]
    te, nv = te8[0, :n_tiles], te8[1, :1]

    def tiles_of(tm):
        return dest[0:2].reshape(2, T // tm, tm).transpose(1, 0, 2)

    xs = _dispatch(off, offe, tiles_of(min(DISP_TM, T)), hn3, n_rows, MOE_TILE)
    ys = _experts(te, nv, xs, w_gate[0].astype(BF16), w_up[0].astype(BF16), w_down[0].astype(BF16), MOE_TILE)
    out = _combine(tiles_of(min(COMB_TM, T)), ys, h, rt, final_norm.reshape(1, D))
    return out.reshape(B, S, D)
```

```python
import functools

import jax
import jax.numpy as jnp
from jax import lax
from jax.experimental import pallas as pl
from jax.experimental.pallas import tpu as pltpu

F32 = jnp.float32
BF16 = jnp.bfloat16

NORM_EPS = 1e-6
NEG_INF = -1e30

LANES = 128
HG_HEADS = 4
HG_DK = 128
FOX_HEADS = 8
FOX_DH = 64
N_GROUPS = 4
PER_GROUP = 4
N_EXPERTS = 16

SUB = 16
SUPER = 128
HG_STEP = 2
MOE_TILE = 512
DISP_TM = 512
COMB_TM = 256
VMEM_LIMIT = 56 * 1024 * 1024


def _cparams(sem):
    return pltpu.CompilerParams(dimension_semantics=sem, vmem_limit_bytes=VMEM_LIMIT)


def _split3(x):
    hi = x.astype(BF16)
    r1 = x - hi.astype(F32)
    mid = r1.astype(BF16)
    lo = (r1 - mid.astype(F32)).astype(BF16)
    return hi, mid, lo


def _dot(a, b):
    return jnp.dot(a, b, preferred_element_type=F32)


def _dot_nt(a, b):
    return lax.dot_general(a, b, (((1,), (1,)), ((), ())), preferred_element_type=F32)


def _dot3(m_bf16, x_f32):
    hi, mid, lo = _split3(x_f32)
    return _dot(m_bf16, hi) + _dot(m_bf16, mid) + _dot(m_bf16, lo)


def _inproj_kernel(x_ref, g_ref, w_ref, u_ref, hgf_ref, foxf_ref, xn_ref, *, n_main, col_chunk):
    x = x_ref[...]
    ms = jnp.mean(x * x, axis=-1, keepdims=True)
    xn_ref[...] = ((x * lax.rsqrt(ms + NORM_EPS)) * g_ref[...]).astype(BF16)
    for j in range(n_main // col_chunk):
        r = _dot(xn_ref[...], w_ref[:, j * col_chunk:(j + 1) * col_chunk])
        u_ref[:, j * col_chunk:(j + 1) * col_chunk] = r.astype(BF16)
    r = _dot(xn_ref[...], w_ref[:, n_main:n_main + 4 * LANES])
    hgf_ref[...] = r
    r = _dot(xn_ref[...], w_ref[:, n_main + 4 * LANES:n_main + 5 * LANES])
    foxf_ref[0] = r.T[0:FOX_HEADS, :]


def _inproj(x2, attn_norm, w_all, B, S, tm):
    T, D = x2.shape
    n_main = 6 * 512
    per_seq = S // tm
    return pl.pallas_call(
        functools.partial(_inproj_kernel, n_main=n_main, col_chunk=512),
        grid=(T // tm,),
        in_specs=[
            pl.BlockSpec((tm, D), lambda i: (i, 0)),
            pl.BlockSpec((1, D), lambda i: (0, 0)),
            pl.BlockSpec(w_all.shape, lambda i: (0, 0)),
        ],
        out_specs=[
            pl.BlockSpec((tm, n_main), lambda i: (i, 0)),
            pl.BlockSpec((tm, 4 * LANES), lambda i: (i, 0)),
            pl.BlockSpec((1, FOX_HEADS, tm), lambda i: (i // per_seq, 0, i % per_seq)),
        ],
        out_shape=[
            jax.ShapeDtypeStruct((T, n_main), BF16),
            jax.ShapeDtypeStruct((T, 4 * LANES), F32),
            jax.ShapeDtypeStruct((B, FOX_HEADS, S), F32),
        ],
        scratch_shapes=[pltpu.VMEM((tm, D), BF16)],
        compiler_params=_cparams(("arbitrary",)),
        name="inproj",
    )(x2, attn_norm, w_all)


def _foxcum_kernel(f_ref, bias_ref, cum_ref, off_ref):
    blk = f_ref.shape[1]

    @pl.when(pl.program_id(0) == 0)
    def _():
        off_ref[...] = jnp.zeros_like(off_ref)

    r_i = lax.broadcasted_iota(jnp.int32, (blk, blk), 0)
    c_i = lax.broadcasted_iota(jnp.int32, (blk, blk), 1)
    upper = (r_i <= c_i).astype(BF16)
    z = f_ref[...] + bias_ref[...]
    logf = jnp.minimum(z, 0.0) - jnp.log(1.0 + jnp.exp(-jnp.abs(z)))
    hi, mid, lo = _split3(logf)
    cum = _dot(hi, upper) + _dot(mid, upper) + _dot(lo, upper) + off_ref[...]
    cum_ref[...] = cum
    off_ref[...] = cum[:, blk - 1:blk]


def _foxcum(foxf2, bias_col, blk):
    R, S = foxf2.shape
    return pl.pallas_call(
        _foxcum_kernel,
        grid=(S // blk,),
        in_specs=[pl.BlockSpec((R, blk), lambda j: (0, j)), pl.BlockSpec((R, 1), lambda j: (0, 0))],
        out_specs=pl.BlockSpec((R, blk), lambda j: (0, j)),
        out_shape=jax.ShapeDtypeStruct((R, S), F32),
        scratch_shapes=[pltpu.VMEM((R, 1), F32)],
        compiler_params=_cparams(("arbitrary",)),
        name="foxcum",
    )(foxf2, bias_col)


def _foxt_kernel(q_ref, k_ref, v_ref, cum_ref, gain_ref, o_ref,
                 kaug_scr, vt_scr, qa_scr, s_scr, p_scr, m_scr, al_scr, acc_scr, *, tq):
    qi = pl.program_id(1)
    nkb = vt_scr.shape[1]
    lane = lax.broadcasted_iota(jnp.int32, (1, LANES), 1)
    rid = lax.broadcasted_iota(jnp.int32, (LANES, 1), 0)
    lo_lane, lo_row = lane < FOX_DH, rid < FOX_DH
    aug_lane = [(lane >= FOX_DH) & (lane < FOX_DH + 3), lane < 3]

    @pl.when(qi == 0)
    def _():
        for kb in range(nkb):
            rows = slice(kb * tq, (kb + 1) * tq)
            for p in range(FOX_HEADS // 2):
                ls = slice(p * LANES, (p + 1) * LANES)
                pe = _split3(cum_ref[0, p, 0, kb:kb + 1, :])
                po = _split3(cum_ref[0, p, 1, kb:kb + 1, :])
                x = jnp.zeros((LANES, tq), F32)
                for j in range(3):
                    x = jnp.where(rid == FOX_DH + j, pe[j].astype(F32), x)
                    x = jnp.where(rid == j, po[j].astype(F32), x)
                aug = x.T
                kf = k_ref[0, rows, ls].astype(F32)
                kaug_scr[2 * p, rows, :] = jnp.where(lo_lane, kf, aug).astype(BF16)
                kaug_scr[2 * p + 1, rows, :] = jnp.where(lo_lane, aug, kf).astype(BF16)
                vt = v_ref[0, rows, ls].astype(F32).T
                vt_scr[2 * p, kb] = jnp.where(lo_row, vt, 1.0).astype(BF16)
                vt_scr[2 * p + 1, kb] = jnp.where(lo_row, 1.0, vt).astype(BF16)

    for p in range(FOX_HEADS // 2):
        q = q_ref[0, :, p * LANES:(p + 1) * LANES].astype(F32) * (FOX_DH ** -0.5)
        qa_scr[2 * p] = jnp.where(lo_lane, q, jnp.where(aug_lane[0], -1.0, 0.0)).astype(BF16)
        qa_scr[2 * p + 1] = jnp.where(lo_lane, jnp.where(aug_lane[1], -1.0, 0.0), q).astype(BF16)
    m_scr[...] = jnp.full(m_scr.shape, NEG_INF, F32)
    acc_scr[...] = jnp.zeros_like(acc_scr)
    krow = lax.broadcasted_iota(jnp.int32, (tq, LANES), 0)
    qcol = lax.broadcasted_iota(jnp.int32, (tq, LANES), 1)

    def step(kb, masked):
        r0 = pl.multiple_of(kb * tq, tq)
        for hd in range(FOX_HEADS):
            s_scr[hd] = _dot_nt(kaug_scr[hd, pl.ds(r0, tq), :], qa_scr[hd])
        for hd in range(FOX_HEADS):
            for lt in range(tq // LANES):
                ls = slice(lt * LANES, (lt + 1) * LANES)
                s = s_scr[hd, :, ls]
                if masked:
                    s = jnp.where(krow <= qcol + lt * LANES, s, NEG_INF)
                m = m_scr[hd:hd + 1, ls]
                m_new = jnp.maximum(m, jnp.max(s, axis=0, keepdims=True))
                p_scr[hd, :, ls] = jnp.exp(s - m_new).astype(BF16)
                al_scr[hd:hd + 1, ls] = jnp.exp(m - m_new)
                m_scr[hd:hd + 1, ls] = m_new
        for hd in range(FOX_HEADS):
            acc_scr[hd] = acc_scr[hd] * al_scr[hd:hd + 1, :] + _dot(vt_scr[hd, kb], p_scr[hd])

    def body(kb, c):
        step(kb, False)
        return c

    lax.fori_loop(0, qi, body, 0)
    step(qi, True)
    for p in range(FOX_HEADS // 2):
        ae, ao = acc_scr[2 * p], acc_scr[2 * p + 1]
        o = jnp.where(lo_row, ae / ae[FOX_DH:FOX_DH + 1, :], ao / ao[0:1, :])
        o2 = o * o
        ms_e = jnp.sum(jnp.where(lo_row, o2, 0.0), axis=0, keepdims=True) * (1.0 / FOX_DH)
        ms_o = jnp.sum(jnp.where(lo_row, 0.0, o2), axis=0, keepdims=True) * (1.0 / FOX_DH)
        o = o * lax.rsqrt(jnp.where(lo_row, ms_e, ms_o) + NORM_EPS)
        o_ref[0, :, p * LANES:(p + 1) * LANES] = (o.T * gain_ref[...]).astype(BF16)


def _foxt(u3, cum5, gain2, tq):
    B, S, _ = u3.shape
    nq = S // tq
    fw = FOX_HEADS * FOX_DH
    qc, kc, vc = 3, 4, 5
    return pl.pallas_call(
        functools.partial(_foxt_kernel, tq=tq),
        grid=(B, nq),
        in_specs=[
            pl.BlockSpec((1, tq, fw), lambda b, i: (b, i, qc)),
            pl.BlockSpec((1, S, fw), lambda b, i: (b, 0, kc)),
            pl.BlockSpec((1, S, fw), lambda b, i: (b, 0, vc)),
            pl.BlockSpec((1, FOX_HEADS // 2, 2, nq, tq), lambda b, i: (b, 0, 0, 0, 0)),
            pl.BlockSpec((1, LANES), lambda b, i: (0, 0)),
        ],
        out_specs=pl.BlockSpec((1, tq, fw), lambda b, i: (b, i, 0)),
        out_shape=jax.ShapeDtypeStruct((B, S, fw), BF16),
        scratch_shapes=[
            pltpu.VMEM((FOX_HEADS, S, LANES), BF16),
            pltpu.VMEM((FOX_HEADS, nq, LANES, tq), BF16),
            pltpu.VMEM((FOX_HEADS, tq, LANES), BF16),
            pltpu.VMEM((FOX_HEADS, tq, tq), F32),
            pltpu.VMEM((FOX_HEADS, tq, tq), BF16),
            pltpu.VMEM((FOX_HEADS, tq), F32),
            pltpu.VMEM((FOX_HEADS, tq), F32),
            pltpu.VMEM((FOX_HEADS, LANES, tq), F32),
        ],
        compiler_params=_cparams(("arbitrary", "arbitrary")),
        name="fox",
    )(u3, u3, u3, cum5, gain2)


def _hgrn2w_kernel(q_ref, i_ref, g_ref, f_ref, lbl_ref, gain_ref, o_ref, b_scr, q_scr, c_scr, st_scr):
    S, W = q_ref.shape[1], q_ref.shape[2]
    nh = W // LANES
    rows_step = HG_STEP * SUPER
    n_sub = SUPER // SUB
    hsl = [slice(h * LANES, (h + 1) * LANES) for h in range(nh)]
    a = lbl_ref[...]
    am = jnp.max(a, axis=0, keepdims=True)
    ea = jnp.exp(a - am)
    lb = ea[0:1, :] / (ea[0:1, :] + ea[1:2, :])

    r_i = lax.broadcasted_iota(jnp.int32, (SUPER, SUPER), 0)
    c_i = lax.broadcasted_iota(jnp.int32, (SUPER, SUPER), 1)
    same = (r_i // SUB) == (c_i // SUB)
    tri = (same & (c_i <= r_i)).astype(BF16)
    blk1 = same.astype(BF16)
    ones = jnp.ones((LANES, LANES), BF16)
    lane = lax.broadcasted_iota(jnp.int32, (1, LANES), 1)
    trow = lax.broadcasted_iota(jnp.int32, (8, W), 0)

    st_scr[...] = jnp.zeros_like(st_scr)

    def superblock(sb, _):
        r0 = pl.multiple_of(sb * rows_step, rows_step)
        hq = 0.5 * q_ref[0, pl.ds(r0, rows_step), :].astype(F32)
        q = hq + hq * jnp.tanh(hq)
        f = lb + (1.0 - lb) * (0.5 + 0.5 * jnp.tanh(0.5 * f_ref[0, pl.ds(r0, rows_step), :]))
        lf2 = jnp.log2(f)
        halves_r = [slice(k * SUPER, (k + 1) * SUPER) for k in range(HG_STEP)]
        b2 = jnp.concatenate([_dot3(tri, lf2[hr]) for hr in halves_r], axis=0)
        bt2 = jnp.concatenate([_dot3(blk1, lf2[hr]) for hr in halves_r], axis=0)
        c2 = b2 - jnp.log2(1.0 - f)
        qt = (q * jnp.exp2(b2)).astype(BF16)
        kh = jnp.exp2(bt2 - c2).astype(BF16)
        gam = jnp.exp2(bt2)
        b_scr[...] = b2
        q_scr[...] = q
        c_scr[...] = c2
        v = i_ref[0, pl.ds(r0, rows_step), :]

        st = [st_scr[h] for h in range(nh)]
        o_inter = [[] for _ in range(nh)]
        for c in range(HG_STEP * n_sub):
            rows = slice(c * SUB, (c + 1) * SUB)
            for h in range(nh):
                o_inter[h].append(_dot_nt(qt[rows, hsl[h]], st[h].astype(BF16)))
                dst = lax.dot_general(v[rows, hsl[h]], kh[rows, hsl[h]], (((0,), (0,)), ((), ())),
                                      preferred_element_type=F32)
                st[h] = gam[c * SUB:c * SUB + 1, hsl[h]] * st[h] + dst
        for h in range(nh):
            st_scr[h] = st[h]

        a_rows = [[] for _ in range(nh)]
        for c in range(HG_STEP * n_sub):
            base = c * SUB
            col0 = base % SUPER
            tiles = []
            lo_half = []
            for j in range(SUB):
                cs = c_scr[base + j:base + j + 1, :]
                halves = []
                for hf in range(2):
                    if hf == 0 and j >= 8:
                        continue
                    t0 = base + 8 * hf
                    p = q_scr[t0:t0 + 8, :] * jnp.exp2(b_scr[t0:t0 + 8, :] - cs)
                    if j // 8 == hf:
                        p = jnp.where(trow >= (j % 8), p, 0.0)
                    halves.append(p)
                if j < 8:
                    tiles.append(jnp.concatenate(halves, axis=0))
                else:
                    lo_half.append(halves[0])
            for m in range(4):
                tiles.append(jnp.concatenate([lo_half[2 * m], lo_half[2 * m + 1]], axis=0))
            pc = jnp.concatenate(tiles, axis=0).astype(BF16)
            for h in range(nh):
                rc = _dot(pc[:, hsl[h]], ones)
                a_top = jnp.zeros((8, LANES), F32)
                a_bot = jnp.zeros((8, LANES), F32)
                for j in range(8):
                    sel = lane == (col0 + j)
                    a_top = jnp.where(sel, rc[16 * j:16 * j + 8, :], a_top)
                    a_bot = jnp.where(sel, rc[16 * j + 8:16 * j + 16, :], a_bot)
                for j in range(8, SUB):
                    sel = lane == (col0 + j)
                    a_bot = jnp.where(sel, rc[128 + 8 * (j - 8):128 + 8 * (j - 8) + 8, :], a_bot)
                a_rows[h] += [a_top, a_bot]

        outs = []
        for h in range(nh):
            parts = []
            for k, hr in enumerate(halves_r):
                tl = a_rows[h][k * 2 * n_sub:(k + 1) * 2 * n_sub]
                a_blk = jnp.concatenate(tl, axis=0).astype(BF16)
                parts.append(_dot(a_blk, v[hr, hsl[h]]))
            o = jnp.concatenate(o_inter[h], axis=0) + jnp.concatenate(parts, axis=0)
            ms = jnp.mean(o * o, axis=-1, keepdims=True)
            outs.append(o * lax.rsqrt(ms + NORM_EPS))
        o = jnp.concatenate(outs, axis=1) * gain_ref[...]
        hg = 0.5 * g_ref[0, pl.ds(r0, rows_step), :].astype(F32)
        o_ref[0, pl.ds(r0, rows_step), :] = (o * (hg + hg * jnp.tanh(hg))).astype(BF16)
        return 0

    lax.fori_loop(0, S // rows_step, superblock, 0)


def _hgrn2w(u3, hgf3, lb_logits, gain_w):
    B, S, _ = u3.shape
    W = HG_HEADS * HG_DK
    return pl.pallas_call(
        _hgrn2w_kernel,
        grid=(B,),
        in_specs=[
            pl.BlockSpec((1, S, W), lambda b: (b, 0, 0)),
            pl.BlockSpec((1, S, W), lambda b: (b, 0, 1)),
            pl.BlockSpec((1, S, W), lambda b: (b, 0, 2)),
            pl.BlockSpec((1, S, W), lambda b: (b, 0, 0)),
            pl.BlockSpec((2, W), lambda b: (0, 0)),
            pl.BlockSpec((1, W), lambda b: (0, 0)),
        ],
        out_specs=pl.BlockSpec((1, S, W), lambda b: (b, 0, 0)),
        out_shape=jax.ShapeDtypeStruct((B, S, W), BF16),
        scratch_shapes=[pltpu.VMEM((HG_STEP * SUPER, W), F32)] * 3 + [pltpu.VMEM((HG_HEADS, LANES, LANES), F32)],
        compiler_params=_cparams(("arbitrary",)),
        name="hgrn2",
    )(u3, u3, u3, hgf3, lb_logits, gain_w)


ROW_SUB = 8


def _rows_to_tiles(ref, x):
    m = x.shape[0]
    for c in range(ROW_SUB):
        ref[pl.ds(c, m, stride=ROW_SUB), :] = x[:, c * LANES:(c + 1) * LANES]


def _tiles_to_rows(ref):
    m = ref.shape[0] // ROW_SUB
    return jnp.concatenate([ref[pl.ds(c, m, stride=ROW_SUB), :] for c in range(ROW_SUB)], axis=1)


def _row_tile(ref, r):
    return ref.at[pl.ds(pl.multiple_of(r * ROW_SUB, ROW_SUB), ROW_SUB)]


def _outproj_kernel(oa_ref, ob_ref, x_ref, wo_ref, g_ref, wr_ref, br_ref, h_ref, hn_ref, rt_ref, r8_ref):
    half = oa_ref.shape[1]
    h = x_ref[...] + _dot(oa_ref[...], wo_ref[0:half, :]) + _dot(ob_ref[...], wo_ref[half:2 * half, :])
    h_ref[...] = h
    ms = jnp.mean(h * h, axis=-1, keepdims=True)
    hn = (h * lax.rsqrt(ms + NORM_EPS)) * g_ref[...]
    _rows_to_tiles(hn_ref, hn)
    hn_hi = hn.astype(BF16)
    hn_lo = (hn - hn_hi.astype(F32)).astype(BF16)
    wr = wr_ref[...]
    wr_hi = wr.astype(BF16)
    wr_lo = (wr - wr_hi.astype(F32)).astype(BF16)
    logits = _dot(hn_hi, wr_hi) + _dot(hn_lo, wr_hi) + _dot(hn_hi, wr_lo) + br_ref[...]
    lt = logits.T
    g = [lt[i:i + 1, :] for i in range(N_GROUPS)]
    gm = jnp.maximum(jnp.maximum(g[0], g[1]), jnp.maximum(g[2], g[3]))
    gsel = jnp.where(g[0] == gm, 0, jnp.where(g[1] == gm, 1, jnp.where(g[2] == gm, 2, 3)))
    pg = 1.0 / (jnp.exp(g[0] - gm) + jnp.exp(g[1] - gm) + jnp.exp(g[2] - gm) + jnp.exp(g[3] - gm))
    e = []
    for i in range(PER_GROUP):
        rows = [lt[N_GROUPS + PER_GROUP * gg + i:N_GROUPS + PER_GROUP * gg + i + 1, :] for gg in range(N_GROUPS)]
        e.append(jnp.where(gsel == 0, rows[0], jnp.where(gsel == 1, rows[1], jnp.where(gsel == 2, rows[2], rows[3]))))
    e1 = jnp.maximum(jnp.maximum(e[0], e[1]), jnp.maximum(e[2], e[3]))
    i1 = jnp.where(e[0] == e1, 0, jnp.where(e[1] == e1, 1, jnp.where(e[2] == e1, 2, 3)))
    ex = [jnp.where(i1 == i, -jnp.inf, e[i]) for i in range(PER_GROUP)]
    e2 = jnp.maximum(jnp.maximum(ex[0], ex[1]), jnp.maximum(ex[2], ex[3]))
    i2 = jnp.where(ex[0] == e2, 0, jnp.where(ex[1] == e2, 1, jnp.where(ex[2] == e2, 2, 3)))
    r = jnp.exp(e2 - e1)
    w1 = pg / (1.0 + r)
    w2 = w1 * r
    x1 = gsel * PER_GROUP + i1
    x2 = gsel * PER_GROUP + i2
    ridx = lax.broadcasted_iota(jnp.int32, (LANES, 1), 0)
    wt = jnp.where(ridx == 0, w1, 0.0) + jnp.where(ridx == 1, w2, 0.0)
    rt_ref[...] = wt.T
    sidx = lax.broadcasted_iota(jnp.int32, (8, 1), 0)
    r8_ref[...] = (jnp.where(sidx == 0, x1, 0) + jnp.where(sidx == 1, x2, 0)).astype(F32)


def _outproj(oa, ob, x2, w_out, ffn_norm, wr, br, tm):
    T, D = x2.shape
    half = oa.shape[1]
    return pl.pallas_call(
        _outproj_kernel,
        grid=(T // tm,),
        in_specs=[
            pl.BlockSpec((tm, half), lambda i: (i, 0)),
            pl.BlockSpec((tm, half), lambda i: (i, 0)),
            pl.BlockSpec((tm, D), lambda i: (i, 0)),
            pl.BlockSpec(w_out.shape, lambda i: (0, 0)),
            pl.BlockSpec((1, D), lambda i: (0, 0)),
            pl.BlockSpec(wr.shape, lambda i: (0, 0)),
            pl.BlockSpec((1, LANES), lambda i: (0, 0)),
        ],
        out_specs=[
            pl.BlockSpec((tm, D), lambda i: (i, 0)),
            pl.BlockSpec((tm * ROW_SUB, LANES), lambda i: (i, 0)),
            pl.BlockSpec((tm, LANES), lambda i: (i, 0)),
            pl.BlockSpec((8, tm), lambda i: (0, i)),
        ],
        out_shape=[
            jax.ShapeDtypeStruct((T, D), F32),
            jax.ShapeDtypeStruct((T * ROW_SUB, LANES), F32),
            jax.ShapeDtypeStruct((T, LANES), F32),
            jax.ShapeDtypeStruct((8, T), F32),
        ],
        compiler_params=_cparams(("arbitrary",)),
        name="outproj",
    )(oa, ob, x2, w_out, ffn_norm, wr, br)


def _rank_kernel(r8_ref, rank_ref, cnt_ref, off_scr):
    blk = r8_ref.shape[1]

    @pl.when(pl.program_id(0) == 0)
    def _():
        off_scr[...] = jnp.zeros_like(off_scr)

    x1, x2 = r8_ref[0:1, :], r8_ref[1:2, :]
    eid = lax.broadcasted_iota(jnp.int32, (N_EXPERTS, 1), 0).astype(F32)
    r_i = lax.broadcasted_iota(jnp.int32, (blk, blk), 0)
    c_i = lax.broadcasted_iota(jnp.int32, (blk, blk), 1)
    upper = (r_i <= c_i).astype(BF16)
    hit = ((x1 == eid) | (x2 == eid)).astype(BF16)
    cum = _dot(hit, upper) + off_scr[...]
    rank1 = jnp.sum(jnp.where(x1 == eid, cum, 0.0), axis=0, keepdims=True) - 1.0
    rank2 = jnp.sum(jnp.where(x2 == eid, cum, 0.0), axis=0, keepdims=True) - 1.0
    sidx = lax.broadcasted_iota(jnp.int32, (8, 1), 0)
    rank_ref[...] = jnp.where(sidx == 0, rank1, 0.0) + jnp.where(sidx == 1, rank2, 0.0)
    off_scr[...] = cum[:, blk - 1:blk]
    cnt_ref[...] = jnp.broadcast_to(cum[:, blk - 1:blk], cnt_ref.shape)


def _rank(r8, blk):
    T = r8.shape[1]
    return pl.pallas_call(
        _rank_kernel,
        grid=(T // blk,),
        in_specs=[pl.BlockSpec((8, blk), lambda j: (0, j))],
        out_specs=[pl.BlockSpec((8, blk), lambda j: (0, j)), pl.BlockSpec((N_EXPERTS, LANES), lambda j: (0, 0))],
        out_shape=[jax.ShapeDtypeStruct((8, T), F32), jax.ShapeDtypeStruct((N_EXPERTS, LANES), F32)],
        scratch_shapes=[pltpu.VMEM((N_EXPERTS, 1), F32)],
        compiler_params=_cparams(("arbitrary",)),
        name="moe_rank",
    )(r8)


def _dest_kernel(r8_ref, rank_ref, cnt_ref, dest_ref, meta_ref, te_ref, *, tile):
    cnt = cnt_ref[...]
    cntp = jnp.floor((cnt + (tile - 1.0)) * (1.0 / tile)) * tile
    r_i = lax.broadcasted_iota(jnp.int32, (N_EXPERTS, N_EXPERTS), 0)
    c_i = lax.broadcasted_iota(jnp.int32, (N_EXPERTS, N_EXPERTS), 1)
    off = _dot3((c_i < r_i).astype(BF16), cntp)
    offe = off + cntp
    x1, x2 = r8_ref[0:1, :], r8_ref[1:2, :]
    eid = lax.broadcasted_iota(jnp.int32, (N_EXPERTS, 1), 0).astype(F32)
    off_c = off[:, 0:1]
    d1 = jnp.sum(jnp.where(x1 == eid, off_c, 0.0), axis=0, keepdims=True) + rank_ref[0:1, :]
    d2 = jnp.sum(jnp.where(x2 == eid, off_c, 0.0), axis=0, keepdims=True) + rank_ref[1:2, :]
    sidx = lax.broadcasted_iota(jnp.int32, (8, 1), 0)
    dest_ref[...] = (jnp.where(sidx == 0, d1, 0.0) + jnp.where(sidx == 1, d2, 0.0)).astype(jnp.int32)
    lane = lax.broadcasted_iota(jnp.int32, (1, LANES), 1)
    meta_ref[...] = jnp.where(lane == 0, off, offe).astype(jnp.int32)
    start = lax.broadcasted_iota(jnp.int32, (1, te_ref.shape[1]), 1).astype(F32) * tile
    te = jnp.sum((start >= offe[:, 0:1]).astype(F32), axis=0, keepdims=True)
    nvalid = offe[N_EXPERTS - 1:N_EXPERTS, 0:1] * (1.0 / tile)
    te_ref[...] = (jnp.where(sidx == 0, jnp.minimum(te, N_EXPERTS - 1.0), 0.0)
                   + jnp.where(sidx == 1, nvalid, 0.0)).astype(jnp.int32)


def _dest(r8, rank, cnt, blk, tile, n_tiles_pad):
    T = r8.shape[1]
    return pl.pallas_call(
        functools.partial(_dest_kernel, tile=float(tile)),
        grid=(T // blk,),
        in_specs=[pl.BlockSpec((8, blk), lambda j: (0, j)), pl.BlockSpec((8, blk), lambda j: (0, j)),
                  pl.BlockSpec((N_EXPERTS, LANES), lambda j: (0, 0))],
        out_specs=[pl.BlockSpec((8, blk), lambda j: (0, j)), pl.BlockSpec((N_EXPERTS, LANES), lambda j: (0, 0)),
                   pl.BlockSpec((8, n_tiles_pad), lambda j: (0, 0))],
        out_shape=[jax.ShapeDtypeStruct((8, T), jnp.int32), jax.ShapeDtypeStruct((N_EXPERTS, LANES), jnp.int32),
                   jax.ShapeDtypeStruct((8, n_tiles_pad), jnp.int32)],
        compiler_params=_cparams(("arbitrary",)),
        name="moe_dest",
    )(r8, rank, cnt)


def _dispatch_kernel(off_ref, offe_ref, dest_ref, hn_ref, xs_hbm, zero_scr, sem, zsem, *, tm, tile):
    i = pl.program_id(0)

    @pl.when(i == 0)
    def _():
        zero_scr[...] = jnp.zeros_like(zero_scr)
        for e in range(N_EXPERTS):
            @pl.when(offe_ref[e] > off_ref[e])
            def _():
                dst = xs_hbm.at[pl.ds(pl.multiple_of((offe_ref[e] - tile) * ROW_SUB, ROW_SUB), tile * ROW_SUB)]
                c = pltpu.make_async_copy(zero_scr, dst, zsem)
                c.start()
                c.wait()

        def fill_unused(j, c):
            dst = xs_hbm.at[pl.ds(pl.multiple_of(j * tile * ROW_SUB, ROW_SUB), tile * ROW_SUB)]
            cp = pltpu.make_async_copy(zero_scr, dst, zsem)
            cp.start()
            cp.wait()
            return c

        lax.fori_loop(offe_ref[N_EXPERTS - 1] // tile, xs_hbm.shape[0] // (tile * ROW_SUB), fill_unused, 0)

    def issue(r, c):
        for s in range(2):
            pltpu.make_async_copy(_row_tile(hn_ref, r), _row_tile(xs_hbm, dest_ref[0, s, r]), sem).start(priority=s)
        return c

    lax.fori_loop(0, tm, issue, 0, unroll=8)

    def drain(r, c):
        for s in range(2):
            pltpu.make_async_copy(_row_tile(hn_ref, 0), _row_tile(xs_hbm, 0), sem).wait()
        return c

    lax.fori_loop(0, tm, drain, 0, unroll=8)


def _dispatch(off, offe, dest3, hn3, n_rows, tile):
    nt, _, tm = dest3.shape
    return pl.pallas_call(
        functools.partial(_dispatch_kernel, tm=tm, tile=tile),
        grid_spec=pltpu.PrefetchScalarGridSpec(
            num_scalar_prefetch=2,
            grid=(nt,),
            in_specs=[pl.BlockSpec((1, 2, tm), lambda i, o, oe: (i, 0, 0), memory_space=pltpu.SMEM),
                      pl.BlockSpec((tm * ROW_SUB, LANES), lambda i, o, oe: (i, 0))],
            out_specs=pl.BlockSpec(memory_space=pl.ANY),
            scratch_shapes=[pltpu.VMEM((tile * ROW_SUB, LANES), F32), pltpu.SemaphoreType.DMA,
                            pltpu.SemaphoreType.DMA],
        ),
        out_shape=jax.ShapeDtypeStruct((n_rows * ROW_SUB, LANES), F32),
        compiler_params=_cparams(("arbitrary",)),
        name="moe_dispatch",
    )(off, offe, dest3, hn3)


def _experts_kernel(te_ref, nv_ref, xs_ref, wg_ref, wu_ref, wd_ref, ys_ref):
    @pl.when(pl.program_id(0) < nv_ref[0])
    def _():
        n_part = 2
        part = xs_ref.shape[0] // n_part
        rows = [pl.ds(k * part, part) for k in range(n_part)]
        x = [_tiles_to_rows(xs_ref.at[rows[k]]).astype(BF16) for k in range(n_part)]
        a = [_dot(x[k], wg_ref[0]) for k in range(n_part)]
        u = [_dot(x[k], wu_ref[0]) for k in range(n_part)]
        hid = [((a[k] * jax.nn.sigmoid(a[k])) * u[k]).astype(BF16) for k in range(n_part)]
        y = [_dot(hid[k], wd_ref[0]) for k in range(n_part)]
        for k in range(n_part):
            _rows_to_tiles(ys_ref.at[rows[k]], y[k])

    @pl.when(pl.program_id(0) >= nv_ref[0])
    def _():
        ys_ref[...] = jnp.zeros_like(ys_ref)


def _experts(te, nv, xs, wg, wu, wd, tile):
    E, D, H = wg.shape
    blk = (tile * ROW_SUB, LANES)

    def row_map(j, te, nv):
        return (j, 0)

    return pl.pallas_call(
        _experts_kernel,
        grid_spec=pltpu.PrefetchScalarGridSpec(
            num_scalar_prefetch=2,
            grid=(xs.shape[0] // blk[0],),
            in_specs=[pl.BlockSpec(blk, row_map),
                      pl.BlockSpec((1, D, H), lambda j, te, nv: (te[j], 0, 0)),
                      pl.BlockSpec((1, D, H), lambda j, te, nv: (te[j], 0, 0)),
                      pl.BlockSpec((1, H, D), lambda j, te, nv: (te[j], 0, 0))],
            out_specs=pl.BlockSpec(blk, row_map),
        ),
        out_shape=jax.ShapeDtypeStruct(xs.shape, F32),
        compiler_params=_cparams(("arbitrary",)),
        name="moe_experts",
    )(te, nv, xs, wg, wu, wd)


def _combine_kernel(dcur_ref, dnxt_ref, ys_hbm, h_ref, rt_ref, fin_ref, o_ref, buf, sem, *, tm):
    i = pl.program_id(0)
    n = pl.num_programs(0)
    slot = i % 2

    def issue(d_ref, sl):
        def body(r, c):
            for s in range(2):
                pltpu.make_async_copy(_row_tile(ys_hbm, d_ref[0, s, r]), _row_tile(buf.at[sl, s], r),
                                      sem.at[sl]).start(priority=s)
            return c
        lax.fori_loop(0, tm, body, 0, unroll=8)

    @pl.when(i == 0)
    def _():
        issue(dcur_ref, 0)

    @pl.when(i + 1 < n)
    def _():
        issue(dnxt_ref, 1 - slot)

    def drain(r, c):
        for s in range(2):
            pltpu.make_async_copy(_row_tile(ys_hbm, 0), _row_tile(buf.at[slot, s], 0), sem.at[slot]).wait()
        return c

    lax.fori_loop(0, tm, drain, 0, unroll=8)
    y = (h_ref[...] + rt_ref[:, 0:1] * _tiles_to_rows(buf.at[slot, 0])
         + rt_ref[:, 1:2] * _tiles_to_rows(buf.at[slot, 1]))
    ms = jnp.mean(y * y, axis=-1, keepdims=True)
    o_ref[...] = (y * lax.rsqrt(ms + NORM_EPS)) * fin_ref[...]


def _combine(dest3, ys, h, rt, final_norm):
    nt, _, tm = dest3.shape
    T, D = h.shape
    return pl.pallas_call(
        functools.partial(_combine_kernel, tm=tm),
        grid=(nt,),
        in_specs=[pl.BlockSpec((1, 2, tm), lambda i: (i, 0, 0), memory_space=pltpu.SMEM),
                  pl.BlockSpec((1, 2, tm), lambda i: (jnp.minimum(i + 1, nt - 1), 0, 0), memory_space=pltpu.SMEM),
                  pl.BlockSpec(memory_space=pl.ANY),
                  pl.BlockSpec((tm, D), lambda i: (i, 0)),
                  pl.BlockSpec((tm, LANES), lambda i: (i, 0)),
                  pl.BlockSpec((1, D), lambda i: (0, 0))],
        out_specs=pl.BlockSpec((tm, D), lambda i: (i, 0)),
        out_shape=jax.ShapeDtypeStruct((T, D), F32),
        scratch_shapes=[pltpu.VMEM((2, 2, tm * ROW_SUB, LANES), F32), pltpu.SemaphoreType.DMA((2,))],
        compiler_params=_cparams(("arbitrary",)),
        name="moe_combine",
    )(dest3, dest3, ys, h, rt, final_norm)


def kernel(x, attn_norm, w_in, hg_lb_logits, hg_norm, fox_f_bias, fox_norm, w_out, ffn_norm,
           w_group, b_group, w_expert, b_expert, w_gate, w_up, w_down, final_norm):
    B, S, D = x.shape
    T = B * S
    assert w_in.shape[0] == 1, "single-layer block"
    hw = HG_HEADS * HG_DK
    fw = FOX_HEADS * FOX_DH
    wi = w_in[0]
    o = [0, hw, 2 * hw, 3 * hw, 4 * hw, 4 * hw + fw, 4 * hw + 2 * fw, 4 * hw + 3 * fw]
    w_all = jnp.concatenate(
        [wi[:, o[0]:o[1]], wi[:, o[2]:o[3]], wi[:, o[3]:o[4]], wi[:, o[4]:o[7]], wi[:, o[1]:o[2]],
         jnp.pad(wi[:, o[7]:], ((0, 0), (0, LANES - FOX_HEADS)))], axis=1).astype(BF16)

    x2 = x.reshape(T, D)
    tm_in = min(512, S)
    u, hgf, foxf = _inproj(x2, attn_norm.reshape(1, D), w_all, B, S, tm_in)

    cum = _foxcum(foxf.reshape(B * FOX_HEADS, S), jnp.tile(fox_f_bias[0], B).reshape(B * FOX_HEADS, 1),
                  min(256, S))
    tq = min(512, S)
    u3 = u.reshape(B, S, -1)
    o_b = _foxt(u3, cum.reshape(B, FOX_HEADS // 2, 2, S // tq, tq),
               jnp.tile(fox_norm[0], 2).reshape(1, LANES), tq)
    o_a = _hgrn2w(u3, hgf.reshape(B, S, hw), hg_lb_logits, jnp.tile(hg_norm[0], HG_HEADS).reshape(1, hw))

    wr = jnp.pad(jnp.concatenate([w_group[0], w_expert[0]], axis=1),
                 ((0, 0), (0, LANES - N_GROUPS - N_EXPERTS)))
    br = jnp.pad(jnp.concatenate([b_group[0], b_expert[0]]), (0, LANES - N_GROUPS - N_EXPERTS)).reshape(1, LANES)
    h, hn3, rt, r8 = _outproj(o_a.reshape(T, hw), o_b.reshape(T, fw), x2, w_out[0].astype(BF16),
                              ffn_norm[0].reshape(1, D), wr, br, min(512, T))

    blk = min(1024, T)
    n_rows = 2 * T + N_EXPERTS * MOE_TILE
    n_tiles = n_rows // MOE_TILE
    rank, cnt = _rank(r8, blk)
    dest, meta, te8 = _dest(r8, rank, cnt, blk, MOE_TILE, -(-n_tiles // LANES) * LANES)
    off, offe = meta[:, 0], meta[:, 1]
    te, nv = te8[0, :n_tiles], te8[1, :1]

    def tiles_of(tm):
        return dest[0:2].reshape(2, T // tm, tm).transpose(1, 0, 2)

    xs = _dispatch(off, offe, tiles_of(min(DISP_TM, T)), hn3, n_rows, MOE_TILE)
    ys = _experts(te, nv, xs, w_gate[0].astype(BF16), w_up[0].astype(BF16), w_down[0].astype(BF16), MOE_TILE)
    out = _combine(tiles_of(min(COMB_TM, T)), ys, h, rt, final_norm.reshape(1, D))
    return out.reshape(B, S, D)
```
